```python
import jax, jax.numpy as jnp
from jax import lax
import numpy as np

D_MODEL = 1024
BATCH = 1
SEQ = 16384
DEPTH = 1

HEAD_DIM = 64
N_HEADS_MOBA = 8
N_HEADS_SB = 8
D_MOBA = N_HEADS_MOBA * HEAD_DIM
D_SB = N_HEADS_SB * HEAD_DIM
MOBA_BLOCK = 256
MOBA_TOP_K = 3
Q_CHUNK = 128
ROPE_THETA = 500000.0
ROPE_DIM = HEAD_DIM // 4
D_FF = 2816
FFN_RES = 0.5
EPS = 1e-6
D_IN = 3 * D_MOBA + 3 * D_SB + 2 * D_MODEL
NEG = -1e30

kernel_name = "hybrid_moba_stickbreaking_macaron"


def rms_norm(x, g):
    x32 = x.astype(jnp.float32)
    y = x32 * lax.rsqrt(jnp.mean(x32 * x32, axis=-1, keepdims=True) + EPS)
    return y.astype(x.dtype) * g


def swiglu(x, w_gate, w_up, w_down):
    return (jax.nn.silu(x @ w_gate) * (x @ w_up)) @ w_down


def partial_rotary(x, positions):
    half = ROPE_DIM // 2
    inv_freq = ROPE_THETA ** (-jnp.arange(half, dtype=jnp.float32) * 2.0 / ROPE_DIM)
    ang = positions.astype(jnp.float32)[:, None, :, None] * inv_freq
    cos = jnp.cos(ang).astype(x.dtype)
    sin = jnp.sin(ang).astype(x.dtype)
    x1, x2, rest = x[..., :half], x[..., half:ROPE_DIM], x[..., ROPE_DIM:]
    return jnp.concatenate([x1 * cos - x2 * sin, x2 * cos + x1 * sin, rest], axis=-1)


def split_heads(x, n):
    b, t, _ = x.shape
    return x.reshape(b, t, n, HEAD_DIM).transpose(0, 2, 1, 3)


def merge_heads(x):
    b, h, t, d = x.shape
    return x.transpose(0, 2, 1, 3).reshape(b, t, h * d)


def moba_attention(q, k, v):
    B, H, T, hd = q.shape
    nb = -(-T // MOBA_BLOCK)
    pad = nb * MOBA_BLOCK - T
    k_pad = jnp.pad(k, ((0, 0), (0, 0), (0, pad), (0, 0)))
    v_pad = jnp.pad(v, ((0, 0), (0, 0), (0, pad), (0, 0)))
    kb = k_pad.reshape(B, H, nb, MOBA_BLOCK, hd)
    vb = v_pad.reshape(B, H, nb, MOBA_BLOCK, hd)
    k_mean = jnp.mean(kb.astype(jnp.float32), axis=3)
    k_sel = min(MOBA_TOP_K, nb)
    scale = hd ** -0.5
    bi = jnp.arange(B)[:, None, None, None]
    hi = jnp.arange(H)[None, :, None, None]
    offs = jnp.arange(MOBA_BLOCK)
    blk_ids = jnp.arange(nb)

    def chunk(c):
        start = c * Q_CHUNK
        qc = lax.dynamic_slice_in_dim(q, start, Q_CHUNK, axis=2)
        t = start + jnp.arange(Q_CHUNK)
        own = start // MOBA_BLOCK
        gate = jnp.einsum('bhqd,bhnd->bhqn', qc.astype(jnp.float32), k_mean)
        gate = jnp.where(blk_ids < own, gate, NEG)
        _, idx = lax.top_k(gate, k_sel)
        valid = idx < own
        ks = kb[bi, hi, idx]
        vs = vb[bi, hi, idx]
        s_sel = jnp.einsum('bhqd,bhqkld->bhqkl', qc, ks).astype(jnp.float32) * scale
        s_sel = jnp.where(valid[..., None], s_sel, NEG).reshape(B, H, Q_CHUNK, k_sel * MOBA_BLOCK)
        k_own = lax.dynamic_index_in_dim(kb, own, axis=2, keepdims=False)
        v_own = lax.dynamic_index_in_dim(vb, own, axis=2, keepdims=False)
        s_own = jnp.einsum('bhqd,bhld->bhql', qc, k_own).astype(jnp.float32) * scale
        pos_own = own * MOBA_BLOCK + offs
        s_own = jnp.where(pos_own[None, :] <= t[:, None], s_own, NEG)
        p = jax.nn.softmax(jnp.concatenate([s_sel, s_own], axis=-1), axis=-1).astype(v.dtype)
        p_sel = p[..., :k_sel * MOBA_BLOCK].reshape(B, H, Q_CHUNK, k_sel, MOBA_BLOCK)
        p_own = p[..., k_sel * MOBA_BLOCK:]
        return (jnp.einsum('bhqkl,bhqkld->bhqd', p_sel, vs)
                + jnp.einsum('bhql,bhld->bhqd', p_own, v_own))

    out = lax.map(chunk, jnp.arange(T // Q_CHUNK))
    return out.transpose(1, 2, 0, 3, 4).reshape(B, H, T, hd)


def stick_breaking_attention(q, k, v):
    B, H, T, hd = q.shape
    scale = hd ** -0.5
    s_pos = jnp.arange(T)

    def chunk(c):
        start = c * Q_CHUNK
        qc = lax.dynamic_slice_in_dim(q, start, Q_CHUNK, axis=2)
        t = start + jnp.arange(Q_CHUNK)
        z = jnp.einsum('bhqd,bhsd->bhqs', qc, k).astype(jnp.float32) * scale
        past = s_pos[None, :] < t[:, None]
        log_fail = jnp.where(past, jax.nn.log_sigmoid(-z), 0.0)
        later = lax.cumsum(log_fail, axis=3, reverse=True) - log_fail
        w = jnp.where(past, jnp.exp(jax.nn.log_sigmoid(z) + later), 0.0)
        return jnp.einsum('bhqs,bhsd->bhqd', w.astype(v.dtype), v)

    out = lax.map(chunk, jnp.arange(T // Q_CHUNK))
    return out.transpose(1, 2, 0, 3, 4).reshape(B, H, T, hd)


def hybrid_mixer(u, positions, w_in, b_gate, w_branch_moba, w_branch_sb, w_out):
    proj = u @ w_in
    cuts = [D_MOBA, 2 * D_MOBA, 3 * D_MOBA, 3 * D_MOBA + D_SB, 3 * D_MOBA + 2 * D_SB,
            3 * D_MOBA + 3 * D_SB, 3 * D_MOBA + 3 * D_SB + D_MODEL]
    q_a, k_a, v_a, q_b, k_b, v_b, g_a, g_b = jnp.split(proj, cuts, axis=-1)
    q_a = partial_rotary(split_heads(q_a, N_HEADS_MOBA), positions)
    k_a = partial_rotary(split_heads(k_a, N_HEADS_MOBA), positions)
    y_a = merge_heads(moba_attention(q_a, k_a, split_heads(v_a, N_HEADS_MOBA)))
    y_b = merge_heads(stick_breaking_attention(split_heads(q_b, N_HEADS_SB),
                                               split_heads(k_b, N_HEADS_SB),
                                               split_heads(v_b, N_HEADS_SB)))
    gates = jax.nn.sigmoid(jnp.concatenate([g_a, g_b], axis=-1) + b_gate)
    gate_a, gate_b = gates[..., :D_MODEL], gates[..., D_MODEL:]
    merged = gate_a * (y_a @ w_branch_moba) + gate_b * (y_b @ w_branch_sb)
    return merged @ w_out


def setup_inputs(seed: int = 0) -> dict:
    key = jax.random.key(seed)
    ks = jax.random.split(key, 18)

    def w(k, shape, fan_in):
        return jax.random.normal(k, shape, jnp.float32) * fan_in ** -0.5

    def gain(k):
        return 1.0 + 0.05 * jax.random.normal(k, (DEPTH, D_MODEL), jnp.float32)

    return {
        "x": jax.random.normal(ks[0], (BATCH, SEQ, D_MODEL), jnp.float32),
        "positions": jnp.broadcast_to(jnp.arange(SEQ, dtype=jnp.int32), (BATCH, SEQ)),
        "ffn1_norm": gain(ks[1]),
        "ffn1_w_gate": w(ks[2], (DEPTH, D_MODEL, D_FF), D_MODEL),
        "ffn1_w_up": w(ks[3], (DEPTH, D_MODEL, D_FF), D_MODEL),
        "ffn1_w_down": w(ks[4], (DEPTH, D_FF, D_MODEL), D_FF),
        "mix_norm": gain(ks[5]),
        "w_in": w(ks[6], (DEPTH, D_MODEL, D_IN), D_MODEL),
        "b_gate": 0.02 * jax.random.normal(ks[7], (DEPTH, 2 * D_MODEL), jnp.float32),
        "w_branch_moba": w(ks[8], (DEPTH, D_MOBA, D_MODEL), D_MOBA),
        "w_branch_sb": w(ks[9], (DEPTH, D_SB, D_MODEL), D_SB),
        "w_out": w(ks[10], (DEPTH, D_MODEL, D_MODEL), D_MODEL),
        "ffn2_norm": gain(ks[11]),
        "ffn2_w_gate": w(ks[12], (DEPTH, D_MODEL, D_FF), D_MODEL),
        "ffn2_w_up": w(ks[13], (DEPTH, D_MODEL, D_FF), D_MODEL),
        "ffn2_w_down": w(ks[14], (DEPTH, D_FF, D_MODEL), D_FF),
        "final_norm": 1.0 + 0.05 * jax.random.normal(ks[15], (D_MODEL,), jnp.float32),
    }


def reference(x, positions, ffn1_norm, ffn1_w_gate, ffn1_w_up, ffn1_w_down, mix_norm, w_in,
              b_gate, w_branch_moba, w_branch_sb, w_out, ffn2_norm, ffn2_w_gate, ffn2_w_up,
              ffn2_w_down, final_norm):
    h = x
    for l in range(DEPTH):
        h = h + FFN_RES * swiglu(rms_norm(h, ffn1_norm[l]), ffn1_w_gate[l], ffn1_w_up[l], ffn1_w_down[l])
        h = h + hybrid_mixer(rms_norm(h, mix_norm[l]), positions, w_in[l], b_gate[l],
                             w_branch_moba[l], w_branch_sb[l], w_out[l])
        h = h + FFN_RES * swiglu(rms_norm(h, ffn2_norm[l]), ffn2_w_gate[l], ffn2_w_up[l], ffn2_w_down[l])
    return rms_norm(h, final_norm)
```

```python
import functools

import jax
import jax.numpy as jnp
from jax import lax
from jax.experimental import pallas as pl
from jax.experimental.pallas import tpu as pltpu

HEAD_DIM = 64
N_HEADS = 8
D_ATT = N_HEADS * HEAD_DIM
BLK = 256
TOP_K = 3
ROPE_THETA = 500000.0
ROPE_DIM = HEAD_DIM // 4
ROPE_HALF = ROPE_DIM // 2
FFN_RES = 0.5
EPS = 1e-6
NEG = -1e30
SCALE = HEAD_DIM ** -0.5
SB_LOG_ZERO = -104.0
VMEM_LIMIT_BYTES = 56 * 1024 * 1024

_NT = (((1,), (1,)), ((), ()))
_TN = (((0,), (0,)), ((), ()))


def _rms(x, g):
    return x * lax.rsqrt(jnp.mean(x * x, axis=-1, keepdims=True) + EPS) * g


def _swiglu(xn, wg_ref, wu_ref, wd_ref):
    xb = xn.astype(jnp.bfloat16)
    g = jnp.dot(xb, wg_ref[...], preferred_element_type=jnp.float32)
    u = jnp.dot(xb, wu_ref[...], preferred_element_type=jnp.float32)
    act = (g * (1.0 / (1.0 + jnp.exp(-g))) * u).astype(jnp.bfloat16)
    return jnp.dot(act, wd_ref[...], preferred_element_type=jnp.float32)


def _ffn_proj_kernel(x_ref, pos_ref, invf_ref, g1_ref, wg_ref, wu_ref, wd_ref, gm_ref,
                     wt_ref, wkb_ref,
                     h_ref, qa_ref, ka_ref, kmean_ref, va_ref, qb_ref, kb_ref, vb_ref):
    x = x_ref[...]
    h = x + FFN_RES * _swiglu(_rms(x, g1_ref[...]), wg_ref, wu_ref, wd_ref)
    h_ref[...] = h
    u = _rms(h, gm_ref[...]).astype(jnp.bfloat16)
    tm = u.shape[0]
    pt = lax.dot_general(wt_ref[...], u, _NT, preferred_element_type=jnp.float32)
    kb = jnp.dot(u, wkb_ref[...], preferred_element_type=jnp.float32)

    ang = invf_ref[...] * pos_ref[...].astype(jnp.float32)
    cos = jnp.cos(ang)[None]
    sin = jnp.sin(ang)[None]

    def rotary(t):
        x1 = t[:, :ROPE_HALF]
        x2 = t[:, ROPE_HALF:ROPE_DIM]
        return jnp.concatenate([x1 * cos - x2 * sin, x2 * cos + x1 * sin, t[:, ROPE_DIM:]], axis=1)

    def heads(lo):
        return pt[lo * D_ATT:(lo + 1) * D_ATT].reshape(N_HEADS, HEAD_DIM, tm)

    qa_ref[:, 0] = rotary(heads(0))
    ka = rotary(heads(1)).reshape(D_ATT, tm).T
    kmean_ref[0] = jnp.mean(ka, axis=0, keepdims=True)
    kab = ka.astype(jnp.bfloat16)
    kbb = kb.astype(jnp.bfloat16)
    for hh in range(N_HEADS):
        ka_ref[hh, 0] = kab[:, hh * HEAD_DIM:(hh + 1) * HEAD_DIM]
        kb_ref[hh, 0] = kbb[:, hh * HEAD_DIM:(hh + 1) * HEAD_DIM]
    va_ref[:, 0] = heads(2).astype(jnp.bfloat16)
    qb_ref[:, 0] = (heads(3) * SCALE).astype(jnp.bfloat16)
    vb_ref[:, 0] = heads(4).astype(jnp.bfloat16)


def _moba_kernel(q_ref, k_ref, v_ref, kmean_ref, o_ref, bias_ref, m_ref, l_ref, acc_ref):
    i = pl.program_id(1)
    nb, tq = bias_ref.shape
    q32 = q_ref[...]
    gate = jnp.dot(kmean_ref[...], q32, preferred_element_type=jnp.float32,
                   precision=lax.Precision.HIGHEST)
    blk = lax.broadcasted_iota(jnp.int32, (nb, tq), 0)
    g = jnp.where(blk < i, gate, NEG)
    sel = jnp.zeros((nb, tq), jnp.bool_)
    for _ in range(TOP_K):
        mx = jnp.max(g, axis=0, keepdims=True)
        first = jnp.min(jnp.where(g == mx, blk, nb), axis=0, keepdims=True)
        pick = blk == first
        sel = sel | pick
        g = jnp.where(pick, -jnp.inf, g)
    bias_ref[...] = jnp.where(sel & (blk < i), 0.0, NEG)

    qs = (q32 * SCALE).astype(jnp.bfloat16)
    row = lax.broadcasted_iota(jnp.int32, (BLK, tq), 0)
    col = lax.broadcasted_iota(jnp.int32, (BLK, tq), 1)
    s = jnp.dot(k_ref[i], qs, preferred_element_type=jnp.float32)
    s = jnp.where(row <= col, s, NEG)
    m = jnp.max(s, axis=0, keepdims=True)
    p = jnp.exp(s - m)
    m_ref[...] = m
    l_ref[...] = jnp.sum(p, axis=0, keepdims=True)
    acc_ref[...] = jnp.dot(v_ref[i], p.astype(jnp.bfloat16), preferred_element_type=jnp.float32)

    def body(j, carry):
        s = jnp.dot(k_ref[j], qs, preferred_element_type=jnp.float32) + bias_ref[pl.ds(j, 1), :]
        m_old = m_ref[...]
        m_new = jnp.maximum(m_old, jnp.max(s, axis=0, keepdims=True))
        alpha = jnp.exp(m_old - m_new)
        p = jnp.exp(s - m_new)
        m_ref[...] = m_new
        l_ref[...] = alpha * l_ref[...] + jnp.sum(p, axis=0, keepdims=True)
        acc_ref[...] = alpha * acc_ref[...] + jnp.dot(
            v_ref[j], p.astype(jnp.bfloat16), preferred_element_type=jnp.float32)
        return carry

    lax.fori_loop(0, i, body, 0)
    o_ref[...] = (acc_ref[...] * (1.0 / l_ref[...])).astype(o_ref.dtype)


def _sb_kernel(q_ref, k_ref, v_ref, o_ref, c_ref, acc_ref):
    i = pl.program_id(1)
    tq = q_ref.shape[-1]
    qs = q_ref[...]
    row = lax.broadcasted_iota(jnp.int32, (BLK, tq), 0)
    col = lax.broadcasted_iota(jnp.int32, (BLK, tq), 1)
    jj = lax.broadcasted_iota(jnp.int32, (BLK, 2 * BLK), 1) % BLK
    ss = lax.broadcasted_iota(jnp.int32, (BLK, 2 * BLK), 0)
    upper = jnp.where(jj > ss, 1.0, 0.0).astype(jnp.bfloat16)

    def block(j, diag):
        z = jnp.dot(k_ref[j], qs, preferred_element_type=jnp.float32)
        log_sig = jnp.minimum(z, 0.0) - jnp.log(1.0 + jnp.exp(-jnp.abs(z)))
        log_fail = log_sig - z
        if diag:
            past = row < col
            log_fail = jnp.where(past, log_fail, 0.0)
        hi = log_fail.astype(jnp.bfloat16)
        lo = (log_fail - hi.astype(jnp.float32)).astype(jnp.bfloat16)
        later = jnp.dot(upper, jnp.concatenate([hi, lo], axis=0),
                        preferred_element_type=jnp.float32)
        c = c_ref[...]
        w = jnp.exp(log_sig + later + c)
        if diag:
            w = jnp.where(past, w, 0.0)
        acc_ref[...] += jnp.dot(v_ref[j], w.astype(jnp.bfloat16), preferred_element_type=jnp.float32)
        c_new = c + jnp.sum(log_fail, axis=0, keepdims=True)
        c_ref[...] = c_new
        return jnp.max(c_new)

    c_ref[...] = jnp.zeros_like(c_ref)
    acc_ref[...] = jnp.zeros_like(acc_ref)
    cmax = block(i, True)

    def cond(state):
        j, cmax = state
        return (j >= 0) & (cmax > SB_LOG_ZERO)

    def body(state):
        j, _ = state
        return j - 1, block(j, False)

    lax.while_loop(cond, body, (i - 1, cmax))
    o_ref[...] = acc_ref[...].astype(o_ref.dtype)


def _merge_ffn_kernel(h_ref, ya_ref, yb_ref, gm_ref, wgate_ref, bg_ref, wa_ref, wb_ref, wo_ref,
                      g2_ref, wg_ref, wu_ref, wd_ref, gf_ref, o_ref):
    h = h_ref[...]
    d = h.shape[-1]
    u = _rms(h, gm_ref[...]).astype(jnp.bfloat16)
    gl = jnp.dot(u, wgate_ref[...], preferred_element_type=jnp.float32) + bg_ref[...]
    gates = 1.0 / (1.0 + jnp.exp(-gl))
    ma = lax.dot_general(ya_ref[...], wa_ref[...], _TN, preferred_element_type=jnp.float32)
    mb = lax.dot_general(yb_ref[...], wb_ref[...], _TN, preferred_element_type=jnp.float32)
    merged = gates[:, :d] * ma + gates[:, d:] * mb
    h = h + jnp.dot(merged.astype(jnp.bfloat16), wo_ref[...], preferred_element_type=jnp.float32)
    h = h + FFN_RES * _swiglu(_rms(h, g2_ref[...]), wg_ref, wu_ref, wd_ref)
    o_ref[...] = _rms(h, gf_ref[...])


def _resident(shape):
    return pl.BlockSpec(shape, lambda *_: (0,) * len(shape), pipeline_mode=pl.Buffered(1))


def kernel(x, positions, ffn1_norm, ffn1_w_gate, ffn1_w_up, ffn1_w_down, mix_norm, w_in, b_gate,
           w_branch_moba, w_branch_sb, w_out, ffn2_norm, ffn2_w_gate, ffn2_w_up, ffn2_w_down,
           final_norm):
    batch, seq, d = x.shape
    assert batch == 1 and ffn1_norm.shape[0] == 1 and seq % BLK == 0 and d == w_out.shape[-1]
    nb = seq // BLK
    d_ff = ffn1_w_gate.shape[-1]
    bf = jnp.bfloat16
    f32 = jnp.float32
    x2 = x[0]
    pos = positions.reshape(1, seq)
    w_in0 = w_in[0]
    a, b = D_ATT, 3 * D_ATT
    w_t = jnp.concatenate([w_in0[:, :b], w_in0[:, b:b + a], w_in0[:, b + 2 * a:b + 3 * a]], axis=1).T.astype(bf)
    w_kb = w_in0[:, b + a:b + 2 * a].astype(bf)
    w_gates = w_in0[:, 2 * b:].astype(bf)
    inv_freq = (ROPE_THETA ** (-jnp.arange(ROPE_HALF, dtype=f32) * 2.0 / ROPE_DIM)).reshape(ROPE_HALF, 1)
    row = lambda v: v.reshape(1, -1)

    cparams = functools.partial(pltpu.CompilerParams, vmem_limit_bytes=VMEM_LIMIT_BYTES)
    tok = lambda i: (i, 0)
    headT = pl.BlockSpec((N_HEADS, 1, HEAD_DIM, BLK), lambda i: (0, i, 0, 0))
    headK = pl.BlockSpec((N_HEADS, 1, BLK, HEAD_DIM), lambda i: (0, i, 0, 0))
    t_shape = (N_HEADS, nb, HEAD_DIM, BLK)
    k_shape = (N_HEADS, nb, BLK, HEAD_DIM)

    h1, qa, ka, kmean, va, qb, kb, vb = pl.pallas_call(
        _ffn_proj_kernel,
        grid=(nb,),
        in_specs=[
            pl.BlockSpec((BLK, d), tok),
            pl.BlockSpec((1, BLK), lambda i: (0, i)),
            _resident((ROPE_HALF, 1)),
            _resident((1, d)),
            _resident((d, d_ff)), _resident((d, d_ff)), _resident((d_ff, d)),
            _resident((1, d)),
            _resident((5 * D_ATT, d)), _resident((d, D_ATT)),
        ],
        out_specs=[
            pl.BlockSpec((BLK, d), tok),
            headT, headK,
            pl.BlockSpec((1, 1, D_ATT), lambda i: (i, 0, 0)),
            headT, headT, headK, headT,
        ],
        out_shape=[
            jax.ShapeDtypeStruct((seq, d), f32),
            jax.ShapeDtypeStruct(t_shape, f32),
            jax.ShapeDtypeStruct(k_shape, bf),
            jax.ShapeDtypeStruct((nb, 1, D_ATT), f32),
            jax.ShapeDtypeStruct(t_shape, bf),
            jax.ShapeDtypeStruct(t_shape, bf),
            jax.ShapeDtypeStruct(k_shape, bf),
            jax.ShapeDtypeStruct(t_shape, bf),
        ],
        compiler_params=cparams(dimension_semantics=("arbitrary",)),
        name="ffn1_proj",
    )(x2, pos, inv_freq, row(ffn1_norm[0]), ffn1_w_gate[0].astype(bf), ffn1_w_up[0].astype(bf),
      ffn1_w_down[0].astype(bf), row(mix_norm[0]), w_t, w_kb)

    kmean_h = kmean.reshape(nb, N_HEADS, HEAD_DIM).transpose(1, 0, 2)

    q_spec = pl.BlockSpec((None, None, HEAD_DIM, BLK), lambda h, i: (h, i, 0, 0))
    k_all = pl.BlockSpec((None, nb, BLK, HEAD_DIM), lambda h, i: (h, 0, 0, 0))
    v_all = pl.BlockSpec((None, nb, HEAD_DIM, BLK), lambda h, i: (h, 0, 0, 0))
    y_spec = pl.BlockSpec((HEAD_DIM, BLK), lambda h, i: (h, i))
    y_shape = jax.ShapeDtypeStruct((D_ATT, seq), bf)

    ya = pl.pallas_call(
        _moba_kernel,
        grid=(N_HEADS, nb),
        in_specs=[q_spec, k_all, v_all, pl.BlockSpec((None, nb, HEAD_DIM), lambda h, i: (h, 0, 0))],
        out_specs=y_spec,
        out_shape=y_shape,
        scratch_shapes=[pltpu.VMEM((nb, BLK), f32), pltpu.VMEM((1, BLK), f32),
                        pltpu.VMEM((1, BLK), f32), pltpu.VMEM((HEAD_DIM, BLK), f32)],
        compiler_params=cparams(dimension_semantics=("arbitrary", "arbitrary")),
        name="moba_attention",
    )(qa, ka, va, kmean_h)

    yb = pl.pallas_call(
        _sb_kernel,
        grid=(N_HEADS, nb),
        in_specs=[q_spec, k_all, v_all],
        out_specs=y_spec,
        out_shape=y_shape,
        scratch_shapes=[pltpu.VMEM((1, BLK), f32), pltpu.VMEM((HEAD_DIM, BLK), f32)],
        compiler_params=cparams(dimension_semantics=("arbitrary", "arbitrary")),
        name="stick_breaking_attention",
    )(qb, kb, vb)

    out = pl.pallas_call(
        _merge_ffn_kernel,
        grid=(nb,),
        in_specs=[
            pl.BlockSpec((BLK, d), tok),
            pl.BlockSpec((D_ATT, BLK), lambda i: (0, i)),
            pl.BlockSpec((D_ATT, BLK), lambda i: (0, i)),
            _resident((1, d)),
            _resident((d, 2 * d)), _resident((1, 2 * d)),
            _resident((D_ATT, d)), _resident((D_ATT, d)), _resident((d, d)),
            _resident((1, d)),
            _resident((d, d_ff)), _resident((d, d_ff)), _resident((d_ff, d)),
            _resident((1, d)),
        ],
        out_specs=pl.BlockSpec((BLK, d), tok),
        out_shape=jax.ShapeDtypeStruct((seq, d), f32),
        compiler_params=cparams(dimension_semantics=("arbitrary",)),
        name="merge_ffn2",
    )(h1, ya, yb, row(mix_norm[0]), w_gates, row(b_gate[0]), w_branch_moba[0].astype(bf),
      w_branch_sb[0].astype(bf), w_out[0].astype(bf), row(ffn2_norm[0]), ffn2_w_gate[0].astype(bf),
      ffn2_w_up[0].astype(bf), ffn2_w_down[0].astype(bf), row(final_norm))
    return out[None]
```

```python
import functools

import jax
import jax.numpy as jnp
from jax import lax
from jax.experimental import pallas as pl
from jax.experimental.pallas import tpu as pltpu

HEAD_DIM = 64
N_HEADS = 8
D_ATT = N_HEADS * HEAD_DIM
LANES = 128
BF16_ROWS = 16
BLK = 256
TOP_K = 3
ROPE_THETA = 500000.0
ROPE_DIM = HEAD_DIM // 4
ROPE_HALF = ROPE_DIM // 2
FFN_RES = 0.5
EPS = 1e-6
NEG = -1e30
SCALE = HEAD_DIM ** -0.5
SB_LOG_ZERO = -104.0
VMEM_LIMIT_BYTES = 56 * 1024 * 1024

_NT = (((1,), (1,)), ((), ()))
_TN = (((0,), (0,)), ((), ()))


def _rms(x, g):
    return x * lax.rsqrt(jnp.mean(x * x, axis=-1, keepdims=True) + EPS) * g


def _swiglu(xn, wg_ref, wu_ref, wd_ref):
    xb = xn.astype(jnp.bfloat16)
    g = jnp.dot(xb, wg_ref[...], preferred_element_type=jnp.float32)
    u = jnp.dot(xb, wu_ref[...], preferred_element_type=jnp.float32)
    act = (g * (1.0 / (1.0 + jnp.exp(-g))) * u).astype(jnp.bfloat16)
    return jnp.dot(act, wd_ref[...], preferred_element_type=jnp.float32)


def _ffn_proj_kernel(x_ref, pos_ref, invf_ref, g1_ref, wg_ref, wu_ref, wd_ref, gm_ref,
                     wt_ref, wkb_ref,
                     h_ref, qa_ref, ka_ref, kmean_ref, va_ref, qb_ref, kb_ref, vb_ref):
    x = x_ref[...]
    h = x + FFN_RES * _swiglu(_rms(x, g1_ref[...]), wg_ref, wu_ref, wd_ref)
    h_ref[...] = h
    u = _rms(h, gm_ref[...]).astype(jnp.bfloat16)
    tm = u.shape[0]
    pt = lax.dot_general(wt_ref[...], u, _NT, preferred_element_type=jnp.float32)
    kb = jnp.dot(u, wkb_ref[...], preferred_element_type=jnp.float32)

    ang = invf_ref[...] * pos_ref[...].astype(jnp.float32)
    cos = jnp.cos(ang)[None]
    sin = jnp.sin(ang)[None]

    def rotary(t):
        x1 = t[:, :ROPE_HALF]
        x2 = t[:, ROPE_HALF:ROPE_DIM]
        return jnp.concatenate([x1 * cos - x2 * sin, x2 * cos + x1 * sin, t[:, ROPE_DIM:]], axis=1)

    def heads(lo):
        return pt[lo * D_ATT:(lo + 1) * D_ATT].reshape(N_HEADS, HEAD_DIM, tm)

    qa_ref[:, 0] = rotary(heads(0))
    ka = rotary(heads(1)).reshape(D_ATT, tm).T
    kmean_ref[0] = jnp.mean(ka, axis=0, keepdims=True)
    ka_ref[...] = ka.astype(jnp.bfloat16)
    kb_ref[...] = kb.astype(jnp.bfloat16)
    va_ref[:, 0] = heads(2).astype(jnp.bfloat16)
    qb_ref[:, 0] = (heads(3) * SCALE).astype(jnp.bfloat16)
    vb_ref[:, 0] = heads(4).astype(jnp.bfloat16)


def _pad_pair(q, hh):
    z = jnp.zeros_like(q)
    return jnp.concatenate([q, z] if hh % 2 == 0 else [z, q], axis=0)


def _k_rows(k_ref, j, nblk, hh):
    p = hh // 2
    return k_ref[pl.ds(pl.multiple_of(j * BLK, BLK), nblk * BLK), p * LANES:(p + 1) * LANES]


def _moba_kernel(q_ref, k_ref, v_ref, kmean_ref, o_ref, qz_ref, bias_ref, m_ref, acc_ref):
    i = pl.program_id(0)
    nb, tq = bias_ref.shape[1:]
    blk = lax.broadcasted_iota(jnp.int32, (nb, tq), 0)
    heads = range(N_HEADS)

    def v_aug(hh, j, nblk):
        ones = jnp.ones((BF16_ROWS, nblk * BLK), jnp.bfloat16)
        return jnp.concatenate(
            [jnp.concatenate([v_ref[hh, j + u] for u in range(nblk)], axis=1), ones], axis=0)

    q32 = [q_ref[hh, 0] for hh in heads]
    gates = [jnp.dot(kmean_ref[hh], q32[hh], preferred_element_type=jnp.float32,
                     precision=lax.Precision.HIGHEST) for hh in heads]
    for hh in heads:
        qz_ref[hh] = _pad_pair((q32[hh] * SCALE).astype(jnp.bfloat16), hh)
    for hh in heads:
        g = jnp.where(blk < i, gates[hh], NEG)
        sel = jnp.zeros((nb, tq), jnp.bool_)
        for _ in range(TOP_K):
            mx = jnp.max(g, axis=0, keepdims=True)
            first = jnp.min(jnp.where(g == mx, blk, nb), axis=0, keepdims=True)
            pick = blk == first
            sel = sel | pick
            g = jnp.where(pick, -jnp.inf, g)
        bias_ref[hh] = jnp.where(sel & (blk < i), 0.0, NEG)

    row = lax.broadcasted_iota(jnp.int32, (BLK, tq), 0)
    col = lax.broadcasted_iota(jnp.int32, (BLK, tq), 1)
    s_own = [jnp.dot(_k_rows(k_ref, i, 1, hh), qz_ref[hh], preferred_element_type=jnp.float32)
             for hh in heads]
    for hh in heads:
        s = jnp.where(row <= col, s_own[hh], NEG)
        m = jnp.max(s, axis=0, keepdims=True)
        p = jnp.exp(s - m)
        m_ref[hh] = m
        acc_ref[hh] = jnp.dot(v_aug(hh, i, 1), p.astype(jnp.bfloat16),
                              preferred_element_type=jnp.float32)

    def attend(j, nblk):
        scores = [jnp.dot(_k_rows(k_ref, j, nblk, hh), qz_ref[hh], preferred_element_type=jnp.float32)
                  for hh in heads]
        for hh in heads:
            s = jnp.concatenate(
                [scores[hh][u * BLK:(u + 1) * BLK] + bias_ref[hh, pl.ds(j + u, 1), :]
                 for u in range(nblk)], axis=0)
            m_old = m_ref[hh]
            m_new = jnp.maximum(m_old, jnp.max(s, axis=0, keepdims=True))
            p = jnp.exp(s - m_new)
            m_ref[hh] = m_new
            acc_ref[hh] = jnp.exp(m_old - m_new) * acc_ref[hh] + jnp.dot(
                v_aug(hh, j, nblk), p.astype(jnp.bfloat16), preferred_element_type=jnp.float32)

    def pair_body(g, carry):
        attend(2 * g, 2)
        return carry

    lax.fori_loop(0, i // 2, pair_body, 0)

    @pl.when(i % 2 == 1)
    def _():
        attend(i - 1, 1)

    for hh in heads:
        acc = acc_ref[hh]
        o_ref[hh * HEAD_DIM:(hh + 1) * HEAD_DIM, :] = (
            acc[:HEAD_DIM] * (1.0 / acc[HEAD_DIM:HEAD_DIM + 1])).astype(o_ref.dtype)


def _sb_kernel(q_ref, k_ref, v_ref, o_ref, qz_ref, c_ref, acc_ref):
    i = pl.program_id(0)
    tq = q_ref.shape[-1]
    row = lax.broadcasted_iota(jnp.int32, (BLK, tq), 0)
    col = lax.broadcasted_iota(jnp.int32, (BLK, tq), 1)
    jj = lax.broadcasted_iota(jnp.int32, (BLK, 2 * BLK), 1) % BLK
    ss = lax.broadcasted_iota(jnp.int32, (BLK, 2 * BLK), 0)
    upper = jnp.where(jj > ss, 1.0, 0.0).astype(jnp.bfloat16)

    heads = range(N_HEADS)

    def block(j, diag):
        zs = [jnp.dot(_k_rows(k_ref, j, 1, hh), qz_ref[hh], preferred_element_type=jnp.float32)
              for hh in heads]
        log_sigs, log_fails, laters = [], [], []
        for hh in heads:
            z = zs[hh]
            log_sig = jnp.minimum(z, 0.0) - jnp.log(1.0 + jnp.exp(-jnp.abs(z)))
            log_fail = log_sig - z
            if diag:
                log_fail = jnp.where(row < col, log_fail, 0.0)
            hi = log_fail.astype(jnp.bfloat16)
            lo = (log_fail - hi.astype(jnp.float32)).astype(jnp.bfloat16)
            laters.append(jnp.dot(upper, jnp.concatenate([hi, lo], axis=0),
                                  preferred_element_type=jnp.float32))
            log_sigs.append(log_sig)
            log_fails.append(log_fail)
        cmax = None
        for hh in heads:
            c = c_ref[hh]
            w = jnp.exp(log_sigs[hh] + laters[hh] + c)
            if diag:
                w = jnp.where(row < col, w, 0.0)
            acc_ref[hh] += jnp.dot(v_ref[hh, j], w.astype(jnp.bfloat16),
                                   preferred_element_type=jnp.float32)
            c_new = c + jnp.sum(log_fails[hh], axis=0, keepdims=True)
            c_ref[hh] = c_new
            cmax = c_new if cmax is None else jnp.maximum(cmax, c_new)
        return jnp.max(cmax)

    for hh in heads:
        qz_ref[hh] = _pad_pair(q_ref[hh, 0], hh)
    c_ref[...] = jnp.zeros_like(c_ref)
    acc_ref[...] = jnp.zeros_like(acc_ref)
    cmax = block(i, True)

    def cond(state):
        j, cmax = state
        return (j >= 0) & (cmax > SB_LOG_ZERO)

    def body(state):
        j, _ = state
        return j - 1, block(j, False)

    lax.while_loop(cond, body, (i - 1, cmax))
    for hh in range(N_HEADS):
        o_ref[hh * HEAD_DIM:(hh + 1) * HEAD_DIM, :] = acc_ref[hh].astype(o_ref.dtype)


def _merge_ffn_kernel(h_ref, ya_ref, yb_ref, gm_ref, wgate_ref, bg_ref, wa_ref, wb_ref, wo_ref,
                      g2_ref, wg_ref, wu_ref, wd_ref, gf_ref, o_ref):
    h = h_ref[...]
    d = h.shape[-1]
    u = _rms(h, gm_ref[...]).astype(jnp.bfloat16)
    gl = jnp.dot(u, wgate_ref[...], preferred_element_type=jnp.float32) + bg_ref[...]
    gates = 1.0 / (1.0 + jnp.exp(-gl))
    ma = lax.dot_general(ya_ref[...], wa_ref[...], _TN, preferred_element_type=jnp.float32)
    mb = lax.dot_general(yb_ref[...], wb_ref[...], _TN, preferred_element_type=jnp.float32)
    merged = gates[:, :d] * ma + gates[:, d:] * mb
    h = h + jnp.dot(merged.astype(jnp.bfloat16), wo_ref[...], preferred_element_type=jnp.float32)
    h = h + FFN_RES * _swiglu(_rms(h, g2_ref[...]), wg_ref, wu_ref, wd_ref)
    o_ref[...] = _rms(h, gf_ref[...])


def _resident(shape):
    return pl.BlockSpec(shape, lambda *_: (0,) * len(shape), pipeline_mode=pl.Buffered(1))


def kernel(x, positions, ffn1_norm, ffn1_w_gate, ffn1_w_up, ffn1_w_down, mix_norm, w_in, b_gate,
           w_branch_moba, w_branch_sb, w_out, ffn2_norm, ffn2_w_gate, ffn2_w_up, ffn2_w_down,
           final_norm):
    batch, seq, d = x.shape
    assert batch == 1 and ffn1_norm.shape[0] == 1 and seq % BLK == 0 and d == w_out.shape[-1]
    nb = seq // BLK
    d_ff = ffn1_w_gate.shape[-1]
    bf = jnp.bfloat16
    f32 = jnp.float32
    x2 = x[0]
    pos = positions.reshape(1, seq)
    w_in0 = w_in[0]
    a, b = D_ATT, 3 * D_ATT
    w_t = jnp.concatenate([w_in0[:, :b], w_in0[:, b:b + a], w_in0[:, b + 2 * a:b + 3 * a]], axis=1).T.astype(bf)
    w_kb = w_in0[:, b + a:b + 2 * a].astype(bf)
    w_gates = w_in0[:, 2 * b:].astype(bf)
    inv_freq = (ROPE_THETA ** (-jnp.arange(ROPE_HALF, dtype=f32) * 2.0 / ROPE_DIM)).reshape(ROPE_HALF, 1)
    row = lambda v: v.reshape(1, -1)

    cparams = functools.partial(pltpu.CompilerParams, vmem_limit_bytes=VMEM_LIMIT_BYTES)
    tok = lambda i: (i, 0)
    head_t = pl.BlockSpec((N_HEADS, 1, HEAD_DIM, BLK), lambda i: (0, i, 0, 0))
    t_shape = (N_HEADS, nb, HEAD_DIM, BLK)

    h1, qa, ka, kmean, va, qb, kb, vb = pl.pallas_call(
        _ffn_proj_kernel,
        grid=(nb,),
        in_specs=[
            pl.BlockSpec((BLK, d), tok),
            pl.BlockSpec((1, BLK), lambda i: (0, i)),
            _resident((ROPE_HALF, 1)),
            _resident((1, d)),
            _resident((d, d_ff)), _resident((d, d_ff)), _resident((d_ff, d)),
            _resident((1, d)),
            _resident((5 * D_ATT, d)), _resident((d, D_ATT)),
        ],
        out_specs=[
            pl.BlockSpec((BLK, d), tok),
            head_t,
            pl.BlockSpec((BLK, D_ATT), tok),
            pl.BlockSpec((1, 1, D_ATT), lambda i: (i, 0, 0)),
            head_t, head_t,
            pl.BlockSpec((BLK, D_ATT), tok),
            head_t,
        ],
        out_shape=[
            jax.ShapeDtypeStruct((seq, d), f32),
            jax.ShapeDtypeStruct(t_shape, f32),
            jax.ShapeDtypeStruct((seq, D_ATT), bf),
            jax.ShapeDtypeStruct((nb, 1, D_ATT), f32),
            jax.ShapeDtypeStruct(t_shape, bf),
            jax.ShapeDtypeStruct(t_shape, bf),
            jax.ShapeDtypeStruct((seq, D_ATT), bf),
            jax.ShapeDtypeStruct(t_shape, bf),
        ],
        compiler_params=cparams(dimension_semantics=("arbitrary",)),
        name="ffn1_proj",
    )(x2, pos, inv_freq, row(ffn1_norm[0]), ffn1_w_gate[0].astype(bf), ffn1_w_up[0].astype(bf),
      ffn1_w_down[0].astype(bf), row(mix_norm[0]), w_t, w_kb)

    kmean_h = kmean.reshape(nb, N_HEADS, HEAD_DIM).transpose(1, 0, 2)

    k_all = _resident((seq, D_ATT))
    v_all = _resident(t_shape)
    y_spec = pl.BlockSpec((D_ATT, BLK), lambda i: (0, i))
    y_shape = jax.ShapeDtypeStruct((D_ATT, seq), bf)
    qz_scratch = pltpu.VMEM((N_HEADS, 2 * HEAD_DIM, BLK), bf)

    ya = pl.pallas_call(
        _moba_kernel,
        grid=(nb,),
        in_specs=[head_t, k_all, v_all, _resident((N_HEADS, nb, HEAD_DIM))],
        out_specs=y_spec,
        out_shape=y_shape,
        scratch_shapes=[qz_scratch,
                        pltpu.VMEM((N_HEADS, nb, BLK), f32),
                        pltpu.VMEM((N_HEADS, 1, BLK), f32),
                        pltpu.VMEM((N_HEADS, HEAD_DIM + BF16_ROWS, BLK), f32)],
        compiler_params=cparams(dimension_semantics=("arbitrary",)),
        name="moba_attention",
    )(qa, ka, va, kmean_h)

    yb = pl.pallas_call(
        _sb_kernel,
        grid=(nb,),
        in_specs=[head_t, k_all, v_all],
        out_specs=y_spec,
        out_shape=y_shape,
        scratch_shapes=[qz_scratch,
                        pltpu.VMEM((N_HEADS, 1, BLK), f32),
                        pltpu.VMEM((N_HEADS, HEAD_DIM, BLK), f32)],
        compiler_params=cparams(dimension_semantics=("arbitrary",)),
        name="stick_breaking_attention",
    )(qb, kb, vb)

    out = pl.pallas_call(
        _merge_ffn_kernel,
        grid=(nb,),
        in_specs=[
            pl.BlockSpec((BLK, d), tok),
            pl.BlockSpec((D_ATT, BLK), lambda i: (0, i)),
            pl.BlockSpec((D_ATT, BLK), lambda i: (0, i)),
            _resident((1, d)),
            _resident((d, 2 * d)), _resident((1, 2 * d)),
            _resident((D_ATT, d)), _resident((D_ATT, d)), _resident((d, d)),
            _resident((1, d)),
            _resident((d, d_ff)), _resident((d, d_ff)), _resident((d_ff, d)),
            _resident((1, d)),
        ],
        out_specs=pl.BlockSpec((BLK, d), tok),
        out_shape=jax.ShapeDtypeStruct((seq, d), f32),
        compiler_params=cparams(dimension_semantics=("arbitrary",)),
        name="merge_ffn2",
    )(h1, ya, yb, row(mix_norm[0]), w_gates, row(b_gate[0]), w_branch_moba[0].astype(bf),
      w_branch_sb[0].astype(bf), w_out[0].astype(bf), row(ffn2_norm[0]), ffn2_w_gate[0].astype(bf),
      ffn2_w_up[0].astype(bf), ffn2_w_down[0].astype(bf), row(final_norm))
    return out[None]
```

```python
import functools

import jax
import jax.numpy as jnp
from jax import lax
from jax.experimental import pallas as pl
from jax.experimental.pallas import tpu as pltpu

HEAD_DIM = 64
N_HEADS = 8
D_ATT = N_HEADS * HEAD_DIM
LANES = 128
BF16_ROWS = 16
BLK = 256
TOP_K = 3
ROPE_THETA = 500000.0
ROPE_DIM = HEAD_DIM // 4
ROPE_HALF = ROPE_DIM // 2
FFN_RES = 0.5
EPS = 1e-6
NEG = -1e30
SCALE = HEAD_DIM ** -0.5
LOG2E = 1.4426950408889634
SB_LOG_ZERO = -104.0
VMEM_LIMIT_BYTES = 56 * 1024 * 1024

_NT = (((1,), (1,)), ((), ()))
_TN = (((0,), (0,)), ((), ()))


def _rms(x, g):
    return x * lax.rsqrt(jnp.mean(x * x, axis=-1, keepdims=True) + EPS) * g


def _swiglu(xn, wg_ref, wu_ref, wd_ref):
    xb = xn.astype(jnp.bfloat16)
    g = jnp.dot(xb, wg_ref[...], preferred_element_type=jnp.float32)
    u = jnp.dot(xb, wu_ref[...], preferred_element_type=jnp.float32)
    act = (g * (1.0 / (1.0 + jnp.exp(-g))) * u).astype(jnp.bfloat16)
    return jnp.dot(act, wd_ref[...], preferred_element_type=jnp.float32)


def _ffn_proj_kernel(x_ref, pos_ref, invf_ref, g1_ref, wg_ref, wu_ref, wd_ref, gm_ref,
                     wt_ref, wkb_ref,
                     h_ref, qa_ref, ka_ref, kmean_ref, va_ref, qb_ref, kb_ref, vb_ref):
    x = x_ref[...]
    h = x + FFN_RES * _swiglu(_rms(x, g1_ref[...]), wg_ref, wu_ref, wd_ref)
    h_ref[...] = h
    u = _rms(h, gm_ref[...]).astype(jnp.bfloat16)
    tm = u.shape[0]
    pt = lax.dot_general(wt_ref[...], u, _NT, preferred_element_type=jnp.float32)
    kb = jnp.dot(u, wkb_ref[...], preferred_element_type=jnp.float32)

    ang = invf_ref[...] * pos_ref[...].astype(jnp.float32)
    cos = jnp.cos(ang)[None]
    sin = jnp.sin(ang)[None]

    def rotary(t):
        x1 = t[:, :ROPE_HALF]
        x2 = t[:, ROPE_HALF:ROPE_DIM]
        return jnp.concatenate([x1 * cos - x2 * sin, x2 * cos + x1 * sin, t[:, ROPE_DIM:]], axis=1)

    def heads(lo):
        return pt[lo * D_ATT:(lo + 1) * D_ATT].reshape(N_HEADS, HEAD_DIM, tm)

    qa_ref[:, 0] = rotary(heads(0))
    ka = rotary(heads(1)).reshape(D_ATT, tm).T
    kmean_ref[0] = jnp.mean(ka, axis=0, keepdims=True)
    ka_ref[...] = ka.astype(jnp.bfloat16)
    kb_ref[...] = kb.astype(jnp.bfloat16)
    va_ref[:, 0] = heads(2).astype(jnp.bfloat16)
    qb_ref[:, 0] = (heads(3) * SCALE).astype(jnp.bfloat16)
    vb_ref[:, 0] = heads(4).astype(jnp.bfloat16)


def _pad_pair(q, hh):
    z = jnp.zeros_like(q)
    return jnp.concatenate([q, z] if hh % 2 == 0 else [z, q], axis=0)


def _k_rows(k_ref, j, nblk, hh):
    p = hh // 2
    return k_ref[pl.ds(pl.multiple_of(j * BLK, BLK), nblk * BLK), p * LANES:(p + 1) * LANES]


def _moba_kernel(q_ref, k_ref, v_ref, kmean_ref, o_ref, qz_ref, bias_ref, m_ref, acc_ref):
    i = pl.program_id(0)
    nb, tq = bias_ref.shape[1:]
    blk = lax.broadcasted_iota(jnp.int32, (nb, tq), 0)
    heads = range(N_HEADS)

    def v_aug(hh, j, nblk):
        ones = jnp.ones((BF16_ROWS, nblk * BLK), jnp.bfloat16)
        return jnp.concatenate(
            [jnp.concatenate([v_ref[hh, j + u] for u in range(nblk)], axis=1), ones], axis=0)

    q32 = [q_ref[hh, 0] for hh in heads]
    gates = [jnp.dot(kmean_ref[hh], q32[hh], preferred_element_type=jnp.float32,
                     precision=lax.Precision.HIGHEST) for hh in heads]
    for hh in heads:
        qz_ref[hh] = _pad_pair((q32[hh] * (SCALE * LOG2E)).astype(jnp.bfloat16), hh)
    for hh in heads:
        g = jnp.where(blk < i, gates[hh], NEG)
        sel = jnp.zeros((nb, tq), jnp.bool_)
        for _ in range(TOP_K):
            mx = jnp.max(g, axis=0, keepdims=True)
            first = jnp.min(jnp.where(g == mx, blk, nb), axis=0, keepdims=True)
            pick = blk == first
            sel = sel | pick
            g = jnp.where(pick, -jnp.inf, g)
        bias_ref[hh] = jnp.where(sel & (blk < i), 0.0, NEG)

    row = lax.broadcasted_iota(jnp.int32, (BLK, tq), 0)
    col = lax.broadcasted_iota(jnp.int32, (BLK, tq), 1)
    s_own = [jnp.dot(_k_rows(k_ref, i, 1, hh), qz_ref[hh], preferred_element_type=jnp.float32)
             for hh in heads]
    for hh in heads:
        s = jnp.where(row <= col, s_own[hh], NEG)
        m = jnp.max(s, axis=0, keepdims=True)
        p = jnp.exp2(s - m)
        m_ref[hh] = m
        acc_ref[hh] = jnp.dot(v_aug(hh, i, 1), p.astype(jnp.bfloat16),
                              preferred_element_type=jnp.float32)

    def attend(j, nblk):
        scores = [jnp.dot(_k_rows(k_ref, j, nblk, hh), qz_ref[hh], preferred_element_type=jnp.float32)
                  for hh in heads]
        for hh in heads:
            parts = [scores[hh][u * BLK:(u + 1) * BLK] for u in range(nblk)]
            biases = [bias_ref[hh, pl.ds(j + u, 1), :] for u in range(nblk)]
            m_old = m_ref[hh]
            m_new = m_old
            for s_u, b_u in zip(parts, biases):
                m_new = jnp.maximum(m_new, jnp.max(s_u, axis=0, keepdims=True) + b_u)
            p = jnp.concatenate([jnp.exp2(s_u - (m_new - b_u)) for s_u, b_u in zip(parts, biases)],
                                axis=0)
            m_ref[hh] = m_new
            acc_ref[hh] = jnp.exp2(m_old - m_new) * acc_ref[hh] + jnp.dot(
                v_aug(hh, j, nblk), p.astype(jnp.bfloat16), preferred_element_type=jnp.float32)

    def pair_body(g, carry):
        attend(2 * g, 2)
        return carry

    lax.fori_loop(0, i // 2, pair_body, 0)

    @pl.when(i % 2 == 1)
    def _():
        attend(i - 1, 1)

    for hh in heads:
        acc = acc_ref[hh]
        o_ref[hh * HEAD_DIM:(hh + 1) * HEAD_DIM, :] = (
            acc[:HEAD_DIM] * (1.0 / acc[HEAD_DIM:HEAD_DIM + 1])).astype(o_ref.dtype)


def _sb_kernel(q_ref, k_ref, v_ref, o_ref, qz_ref, c_ref, acc_ref):
    i = pl.program_id(0)
    tq = q_ref.shape[-1]
    row = lax.broadcasted_iota(jnp.int32, (BLK, tq), 0)
    col = lax.broadcasted_iota(jnp.int32, (BLK, tq), 1)
    jj = lax.broadcasted_iota(jnp.int32, (BLK, 2 * BLK), 1) % BLK
    ss = lax.broadcasted_iota(jnp.int32, (BLK, 2 * BLK), 0)
    upper = jnp.where(jj > ss, 1.0, 0.0).astype(jnp.bfloat16)

    heads = range(N_HEADS)

    def block(j, diag):
        zs = [jnp.dot(_k_rows(k_ref, j, 1, hh), qz_ref[hh], preferred_element_type=jnp.float32)
              for hh in heads]
        log_sigs, log_fails, laters = [], [], []
        for hh in heads:
            z = zs[hh]
            log_sig = jnp.minimum(z, 0.0) - jnp.log(1.0 + jnp.exp(-jnp.abs(z)))
            log_fail = log_sig - z
            if diag:
                log_fail = jnp.where(row < col, log_fail, 0.0)
            hi = log_fail.astype(jnp.bfloat16)
            lo = (log_fail - hi.astype(jnp.float32)).astype(jnp.bfloat16)
            laters.append(jnp.dot(upper, jnp.concatenate([hi, lo], axis=0),
                                  preferred_element_type=jnp.float32))
            log_sigs.append(log_sig)
            log_fails.append(log_fail)
        cmax = None
        for hh in heads:
            c = c_ref[hh]
            w = jnp.exp(log_sigs[hh] + laters[hh] + c)
            if diag:
                w = jnp.where(row < col, w, 0.0)
            acc_ref[hh] += jnp.dot(v_ref[hh, j], w.astype(jnp.bfloat16),
                                   preferred_element_type=jnp.float32)
            c_new = c + jnp.sum(log_fails[hh], axis=0, keepdims=True)
            c_ref[hh] = c_new
            cmax = c_new if cmax is None else jnp.maximum(cmax, c_new)
        return jnp.max(cmax)

    for hh in heads:
        qz_ref[hh] = _pad_pair(q_ref[hh, 0], hh)
    c_ref[...] = jnp.zeros_like(c_ref)
    acc_ref[...] = jnp.zeros_like(acc_ref)
    cmax = block(i, True)

    def cond(state):
        j, cmax = state
        return (j >= 0) & (cmax > SB_LOG_ZERO)

    def body(state):
        j, _ = state
        return j - 1, block(j, False)

    lax.while_loop(cond, body, (i - 1, cmax))
    for hh in range(N_HEADS):
        o_ref[hh * HEAD_DIM:(hh + 1) * HEAD_DIM, :] = acc_ref[hh].astype(o_ref.dtype)


def _merge_ffn_kernel(h_ref, ya_ref, yb_ref, gm_ref, wgate_ref, bg_ref, wa_ref, wb_ref, wo_ref,
                      g2_ref, wg_ref, wu_ref, wd_ref, gf_ref, o_ref):
    h = h_ref[...]
    d = h.shape[-1]
    u = _rms(h, gm_ref[...]).astype(jnp.bfloat16)
    gl = jnp.dot(u, wgate_ref[...], preferred_element_type=jnp.float32) + bg_ref[...]
    gates = 1.0 / (1.0 + jnp.exp(-gl))
    ma = lax.dot_general(ya_ref[...], wa_ref[...], _TN, preferred_element_type=jnp.float32)
    mb = lax.dot_general(yb_ref[...], wb_ref[...], _TN, preferred_element_type=jnp.float32)
    merged = gates[:, :d] * ma + gates[:, d:] * mb
    h = h + jnp.dot(merged.astype(jnp.bfloat16), wo_ref[...], preferred_element_type=jnp.float32)
    h = h + FFN_RES * _swiglu(_rms(h, g2_ref[...]), wg_ref, wu_ref, wd_ref)
    o_ref[...] = _rms(h, gf_ref[...])


def _resident(shape):
    return pl.BlockSpec(shape, lambda *_: (0,) * len(shape), pipeline_mode=pl.Buffered(1))


def kernel(x, positions, ffn1_norm, ffn1_w_gate, ffn1_w_up, ffn1_w_down, mix_norm, w_in, b_gate,
           w_branch_moba, w_branch_sb, w_out, ffn2_norm, ffn2_w_gate, ffn2_w_up, ffn2_w_down,
           final_norm):
    batch, seq, d = x.shape
    assert batch == 1 and ffn1_norm.shape[0] == 1 and seq % BLK == 0 and d == w_out.shape[-1]
    nb = seq // BLK
    d_ff = ffn1_w_gate.shape[-1]
    bf = jnp.bfloat16
    f32 = jnp.float32
    x2 = x[0]
    pos = positions.reshape(1, seq)
    w_in0 = w_in[0]
    a, b = D_ATT, 3 * D_ATT
    w_t = jnp.concatenate([w_in0[:, :b], w_in0[:, b:b + a], w_in0[:, b + 2 * a:b + 3 * a]], axis=1).T.astype(bf)
    w_kb = w_in0[:, b + a:b + 2 * a].astype(bf)
    w_gates = w_in0[:, 2 * b:].astype(bf)
    inv_freq = (ROPE_THETA ** (-jnp.arange(ROPE_HALF, dtype=f32) * 2.0 / ROPE_DIM)).reshape(ROPE_HALF, 1)
    row = lambda v: v.reshape(1, -1)

    cparams = functools.partial(pltpu.CompilerParams, vmem_limit_bytes=VMEM_LIMIT_BYTES)
    tok = lambda i: (i, 0)
    head_t = pl.BlockSpec((N_HEADS, 1, HEAD_DIM, BLK), lambda i: (0, i, 0, 0))
    t_shape = (N_HEADS, nb, HEAD_DIM, BLK)

    h1, qa, ka, kmean, va, qb, kb, vb = pl.pallas_call(
        _ffn_proj_kernel,
        grid=(nb,),
        in_specs=[
            pl.BlockSpec((BLK, d), tok),
            pl.BlockSpec((1, BLK), lambda i: (0, i)),
            _resident((ROPE_HALF, 1)),
            _resident((1, d)),
            _resident((d, d_ff)), _resident((d, d_ff)), _resident((d_ff, d)),
            _resident((1, d)),
            _resident((5 * D_ATT, d)), _resident((d, D_ATT)),
        ],
        out_specs=[
            pl.BlockSpec((BLK, d), tok),
            head_t,
            pl.BlockSpec((BLK, D_ATT), tok),
            pl.BlockSpec((1, 1, D_ATT), lambda i: (i, 0, 0)),
            head_t, head_t,
            pl.BlockSpec((BLK, D_ATT), tok),
            head_t,
        ],
        out_shape=[
            jax.ShapeDtypeStruct((seq, d), f32),
            jax.ShapeDtypeStruct(t_shape, f32),
            jax.ShapeDtypeStruct((seq, D_ATT), bf),
            jax.ShapeDtypeStruct((nb, 1, D_ATT), f32),
            jax.ShapeDtypeStruct(t_shape, bf),
            jax.ShapeDtypeStruct(t_shape, bf),
            jax.ShapeDtypeStruct((seq, D_ATT), bf),
            jax.ShapeDtypeStruct(t_shape, bf),
        ],
        compiler_params=cparams(dimension_semantics=("arbitrary",)),
        name="ffn1_proj",
    )(x2, pos, inv_freq, row(ffn1_norm[0]), ffn1_w_gate[0].astype(bf), ffn1_w_up[0].astype(bf),
      ffn1_w_down[0].astype(bf), row(mix_norm[0]), w_t, w_kb)

    kmean_h = kmean.reshape(nb, N_HEADS, HEAD_DIM).transpose(1, 0, 2)

    k_all = _resident((seq, D_ATT))
    v_all = _resident(t_shape)
    y_spec = pl.BlockSpec((D_ATT, BLK), lambda i: (0, i))
    y_shape = jax.ShapeDtypeStruct((D_ATT, seq), bf)
    qz_scratch = pltpu.VMEM((N_HEADS, 2 * HEAD_DIM, BLK), bf)

    ya = pl.pallas_call(
        _moba_kernel,
        grid=(nb,),
        in_specs=[head_t, k_all, v_all, _resident((N_HEADS, nb, HEAD_DIM))],
        out_specs=y_spec,
        out_shape=y_shape,
        scratch_shapes=[qz_scratch,
                        pltpu.VMEM((N_HEADS, nb, BLK), f32),
                        pltpu.VMEM((N_HEADS, 1, BLK), f32),
                        pltpu.VMEM((N_HEADS, HEAD_DIM + BF16_ROWS, BLK), f32)],
        compiler_params=cparams(dimension_semantics=("arbitrary",)),
        name="moba_attention",
    )(qa, ka, va, kmean_h)

    yb = pl.pallas_call(
        _sb_kernel,
        grid=(nb,),
        in_specs=[head_t, k_all, v_all],
        out_specs=y_spec,
        out_shape=y_shape,
        scratch_shapes=[qz_scratch,
                        pltpu.VMEM((N_HEADS, 1, BLK), f32),
                        pltpu.VMEM((N_HEADS, HEAD_DIM, BLK), f32)],
        compiler_params=cparams(dimension_semantics=("arbitrary",)),
        name="stick_breaking_attention",
    )(qb, kb, vb)

    out = pl.pallas_call(
        _merge_ffn_kernel,
        grid=(nb,),
        in_specs=[
            pl.BlockSpec((BLK, d), tok),
            pl.BlockSpec((D_ATT, BLK), lambda i: (0, i)),
            pl.BlockSpec((D_ATT, BLK), lambda i: (0, i)),
            _resident((1, d)),
            _resident((d, 2 * d)), _resident((1, 2 * d)),
            _resident((D_ATT, d)), _resident((D_ATT, d)), _resident((d, d)),
            _resident((1, d)),
            _resident((d, d_ff)), _resident((d, d_ff)), _resident((d_ff, d)),
            _resident((1, d)),
        ],
        out_specs=pl.BlockSpec((BLK, d), tok),
        out_shape=jax.ShapeDtypeStruct((seq, d), f32),
        compiler_params=cparams(dimension_semantics=("arbitrary",)),
        name="merge_ffn2",
    )(h1, ya, yb, row(mix_norm[0]), w_gates, row(b_gate[0]), w_branch_moba[0].astype(bf),
      w_branch_sb[0].astype(bf), w_out[0].astype(bf), row(ffn2_norm[0]), ffn2_w_gate[0].astype(bf),
      ffn2_w_up[0].astype(bf), ffn2_w_down[0].astype(bf), row(final_norm))
    return out[None]
```

```python
import functools

import jax
import jax.numpy as jnp
from jax import lax
from jax.experimental import pallas as pl
from jax.experimental.pallas import tpu as pltpu

HEAD_DIM = 64
N_HEADS = 8
D_ATT = N_HEADS * HEAD_DIM
LANES = 128
BF16_ROWS = 16
BLK = 256
TOP_K = 3
GROUP = 2
ROPE_THETA = 500000.0
ROPE_DIM = HEAD_DIM // 4
ROPE_HALF = ROPE_DIM // 2
FFN_RES = 0.5
EPS = 1e-6
NEG = -1e30
SCALE = HEAD_DIM ** -0.5
LOG2E = 1.4426950408889634
SB_LOG_ZERO = -104.0
VMEM_LIMIT_BYTES = 56 * 1024 * 1024

_NT = (((1,), (1,)), ((), ()))
_TN = (((0,), (0,)), ((), ()))


def _rms(x, g):
    return x * lax.rsqrt(jnp.mean(x * x, axis=-1, keepdims=True) + EPS) * g


def _swiglu(xn, wg_ref, wu_ref, wd_ref):
    xb = xn.astype(jnp.bfloat16)
    g = jnp.dot(xb, wg_ref[...], preferred_element_type=jnp.float32)
    u = jnp.dot(xb, wu_ref[...], preferred_element_type=jnp.float32)
    act = (g * (1.0 / (1.0 + jnp.exp(-g))) * u).astype(jnp.bfloat16)
    return jnp.dot(act, wd_ref[...], preferred_element_type=jnp.float32)


def _ffn_proj_kernel(x_ref, pos_ref, invf_ref, g1_ref, wg_ref, wu_ref, wd_ref, gm_ref,
                     wt_ref, wkb_ref,
                     h_ref, qa_ref, ka_ref, kmean_ref, va_ref, qb_ref, kb_ref, vb_ref):
    x = x_ref[...]
    h = x + FFN_RES * _swiglu(_rms(x, g1_ref[...]), wg_ref, wu_ref, wd_ref)
    h_ref[...] = h
    u = _rms(h, gm_ref[...]).astype(jnp.bfloat16)
    tm = u.shape[0]
    pt = lax.dot_general(wt_ref[...], u, _NT, preferred_element_type=jnp.float32)
    kb = jnp.dot(u, wkb_ref[...], preferred_element_type=jnp.float32)

    ang = invf_ref[...] * pos_ref[...].astype(jnp.float32)
    cos = jnp.cos(ang)[None]
    sin = jnp.sin(ang)[None]

    def rotary(t):
        x1 = t[:, :ROPE_HALF]
        x2 = t[:, ROPE_HALF:ROPE_DIM]
        return jnp.concatenate([x1 * cos - x2 * sin, x2 * cos + x1 * sin, t[:, ROPE_DIM:]], axis=1)

    def heads(lo):
        return pt[lo * D_ATT:(lo + 1) * D_ATT].reshape(N_HEADS, HEAD_DIM, tm)

    qa_ref[:, 0] = rotary(heads(0))
    ka = rotary(heads(1)).reshape(D_ATT, tm).T
    kmean_ref[0] = jnp.mean(ka, axis=0, keepdims=True)
    ka_ref[...] = ka.astype(jnp.bfloat16)
    kb_ref[...] = kb.astype(jnp.bfloat16)
    va_ref[:, 0] = heads(2).astype(jnp.bfloat16)
    qb_ref[:, 0] = (heads(3) * SCALE).astype(jnp.bfloat16)
    vb_ref[:, 0] = heads(4).astype(jnp.bfloat16)


def _pad_pair(q, hh):
    z = jnp.zeros_like(q)
    return jnp.concatenate([q, z] if hh % 2 == 0 else [z, q], axis=0)


def _k_rows(k_ref, j, nblk, hh):
    p = hh // 2
    return k_ref[pl.ds(pl.multiple_of(j * BLK, BLK), nblk * BLK), p * LANES:(p + 1) * LANES]


def _moba_kernel(q_ref, k_ref, v_ref, kmean_ref, o_ref, qz_ref, bias_ref, m_ref, acc_ref,
                 sa_ref, sb_ref):
    i = pl.program_id(0)
    nb, tq = bias_ref.shape[1:]
    blk = lax.broadcasted_iota(jnp.int32, (nb, tq), 0)
    heads = range(N_HEADS)

    def v_aug(hh, j, nblk):
        ones = jnp.ones((BF16_ROWS, nblk * BLK), jnp.bfloat16)
        return jnp.concatenate(
            [jnp.concatenate([v_ref[hh, j + u] for u in range(nblk)], axis=1), ones], axis=0)

    q32 = [q_ref[hh, 0] for hh in heads]
    gates = [jnp.dot(kmean_ref[hh], q32[hh], preferred_element_type=jnp.float32,
                     precision=lax.Precision.HIGHEST) for hh in heads]
    for hh in heads:
        qz_ref[hh] = _pad_pair((q32[hh] * (SCALE * LOG2E)).astype(jnp.bfloat16), hh)
    for hh in heads:
        g = jnp.where(blk < i, gates[hh], NEG)
        sel = jnp.zeros((nb, tq), jnp.bool_)
        for _ in range(TOP_K):
            mx = jnp.max(g, axis=0, keepdims=True)
            first = jnp.min(jnp.where(g == mx, blk, nb), axis=0, keepdims=True)
            pick = blk == first
            sel = sel | pick
            g = jnp.where(pick, -jnp.inf, g)
        bias_ref[hh] = jnp.where(sel & (blk < i), 0.0, NEG)

    row = lax.broadcasted_iota(jnp.int32, (BLK, tq), 0)
    col = lax.broadcasted_iota(jnp.int32, (BLK, tq), 1)
    s_own = [jnp.dot(_k_rows(k_ref, i, 1, hh), qz_ref[hh], preferred_element_type=jnp.float32)
             for hh in heads]
    for hh in heads:
        s = jnp.where(row <= col, s_own[hh], NEG)
        m = jnp.max(s, axis=0, keepdims=True)
        p = jnp.exp2(s - m)
        m_ref[hh] = m
        acc_ref[hh] = jnp.dot(v_aug(hh, i, 1), p.astype(jnp.bfloat16),
                              preferred_element_type=jnp.float32)

    n_groups_max = nb // GROUP

    def scores_into(dst_ref, grp, hh):
        dst_ref[hh] = jnp.dot(_k_rows(k_ref, grp * GROUP, GROUP, hh), qz_ref[hh],
                              preferred_element_type=jnp.float32)

    def probs(src_ref, grp, hh):
        j = grp * GROUP
        parts = [src_ref[hh, u * BLK:(u + 1) * BLK] for u in range(GROUP)]
        biases = [bias_ref[hh, pl.ds(j + u, 1), :] for u in range(GROUP)]
        m_old = m_ref[hh]
        m_new = m_old
        for s_u, b_u in zip(parts, biases):
            m_new = jnp.maximum(m_new, jnp.max(s_u, axis=0, keepdims=True) + b_u)
        p = jnp.concatenate([jnp.exp2((s_u - (m_new - b_u)).astype(jnp.bfloat16))
                             for s_u, b_u in zip(parts, biases)], axis=0)
        m_ref[hh] = m_new
        return hh, j, jnp.exp2(m_old - m_new), p

    def accumulate(hh, j, alpha, p):
        acc_ref[hh] = alpha * acc_ref[hh] + jnp.dot(v_aug(hh, j, GROUP), p,
                                                   preferred_element_type=jnp.float32)

    for hh in heads:
        scores_into(sa_ref, 0, hh)

    def two_groups(t, carry):
        g0 = 2 * t
        nxt = jnp.minimum(g0 + 2, n_groups_max - 1)
        pending = None
        for src_ref, dst_ref, grp, grp_next in ((sa_ref, sb_ref, g0, g0 + 1),
                                                (sb_ref, sa_ref, g0 + 1, nxt)):
            for hh in heads:
                scores_into(dst_ref, grp_next, hh)
                ready = probs(src_ref, grp, hh)
                if pending is not None:
                    accumulate(*pending)
                pending = ready
        accumulate(*pending)
        return carry

    n_groups = (i + GROUP - 1) // GROUP
    lax.fori_loop(0, (n_groups + 1) // 2, two_groups, 0)

    for hh in heads:
        acc = acc_ref[hh]
        o_ref[hh * HEAD_DIM:(hh + 1) * HEAD_DIM, :] = (
            acc[:HEAD_DIM] * (1.0 / acc[HEAD_DIM:HEAD_DIM + 1])).astype(o_ref.dtype)


def _sb_kernel(q_ref, k_ref, v_ref, o_ref, qz_ref, c_ref, acc_ref):
    i = pl.program_id(0)
    tq = q_ref.shape[-1]
    row = lax.broadcasted_iota(jnp.int32, (BLK, tq), 0)
    col = lax.broadcasted_iota(jnp.int32, (BLK, tq), 1)
    jj = lax.broadcasted_iota(jnp.int32, (BLK, 2 * BLK), 1) % BLK
    ss = lax.broadcasted_iota(jnp.int32, (BLK, 2 * BLK), 0)
    upper = jnp.where(jj > ss, 1.0, 0.0).astype(jnp.bfloat16)

    heads = range(N_HEADS)

    def block(j, diag):
        zs = [jnp.dot(_k_rows(k_ref, j, 1, hh), qz_ref[hh], preferred_element_type=jnp.float32)
              for hh in heads]
        log_sigs, log_fails, laters = [], [], []
        for hh in heads:
            z = zs[hh]
            log_sig = jnp.minimum(z, 0.0) - jnp.log(1.0 + jnp.exp(-jnp.abs(z)))
            log_fail = log_sig - z
            if diag:
                log_fail = jnp.where(row < col, log_fail, 0.0)
            hi = log_fail.astype(jnp.bfloat16)
            lo = (log_fail - hi.astype(jnp.float32)).astype(jnp.bfloat16)
            laters.append(jnp.dot(upper, jnp.concatenate([hi, lo], axis=0),
                                  preferred_element_type=jnp.float32))
            log_sigs.append(log_sig)
            log_fails.append(log_fail)
        cmax = None
        for hh in heads:
            c = c_ref[hh]
            w = jnp.exp(log_sigs[hh] + laters[hh] + c)
            if diag:
                w = jnp.where(row < col, w, 0.0)
            acc_ref[hh] += jnp.dot(v_ref[hh, j], w.astype(jnp.bfloat16),
                                   preferred_element_type=jnp.float32)
            c_new = c + jnp.sum(log_fails[hh], axis=0, keepdims=True)
            c_ref[hh] = c_new
            cmax = c_new if cmax is None else jnp.maximum(cmax, c_new)
        return jnp.max(cmax)

    for hh in heads:
        qz_ref[hh] = _pad_pair(q_ref[hh, 0], hh)
    c_ref[...] = jnp.zeros_like(c_ref)
    acc_ref[...] = jnp.zeros_like(acc_ref)
    cmax = block(i, True)

    def cond(state):
        j, cmax = state
        return (j >= 0) & (cmax > SB_LOG_ZERO)

    def body(state):
        j, _ = state
        return j - 1, block(j, False)

    lax.while_loop(cond, body, (i - 1, cmax))
    for hh in range(N_HEADS):
        o_ref[hh * HEAD_DIM:(hh + 1) * HEAD_DIM, :] = acc_ref[hh].astype(o_ref.dtype)


def _merge_ffn_kernel(h_ref, ya_ref, yb_ref, gm_ref, wgate_ref, bg_ref, wa_ref, wb_ref, wo_ref,
                      g2_ref, wg_ref, wu_ref, wd_ref, gf_ref, o_ref):
    h = h_ref[...]
    d = h.shape[-1]
    u = _rms(h, gm_ref[...]).astype(jnp.bfloat16)
    gl = jnp.dot(u, wgate_ref[...], preferred_element_type=jnp.float32) + bg_ref[...]
    gates = 1.0 / (1.0 + jnp.exp(-gl))
    ma = lax.dot_general(ya_ref[...], wa_ref[...], _TN, preferred_element_type=jnp.float32)
    mb = lax.dot_general(yb_ref[...], wb_ref[...], _TN, preferred_element_type=jnp.float32)
    merged = gates[:, :d] * ma + gates[:, d:] * mb
    h = h + jnp.dot(merged.astype(jnp.bfloat16), wo_ref[...], preferred_element_type=jnp.float32)
    h = h + FFN_RES * _swiglu(_rms(h, g2_ref[...]), wg_ref, wu_ref, wd_ref)
    o_ref[...] = _rms(h, gf_ref[...])


def _resident(shape):
    return pl.BlockSpec(shape, lambda *_: (0,) * len(shape), pipeline_mode=pl.Buffered(1))


def kernel(x, positions, ffn1_norm, ffn1_w_gate, ffn1_w_up, ffn1_w_down, mix_norm, w_in, b_gate,
           w_branch_moba, w_branch_sb, w_out, ffn2_norm, ffn2_w_gate, ffn2_w_up, ffn2_w_down,
           final_norm):
    batch, seq, d = x.shape
    assert batch == 1 and ffn1_norm.shape[0] == 1 and seq % BLK == 0 and d == w_out.shape[-1]
    nb = seq // BLK
    d_ff = ffn1_w_gate.shape[-1]
    bf = jnp.bfloat16
    f32 = jnp.float32
    x2 = x[0]
    pos = positions.reshape(1, seq)
    w_in0 = w_in[0]
    a, b = D_ATT, 3 * D_ATT
    w_t = jnp.concatenate([w_in0[:, :b], w_in0[:, b:b + a], w_in0[:, b + 2 * a:b + 3 * a]], axis=1).T.astype(bf)
    w_kb = w_in0[:, b + a:b + 2 * a].astype(bf)
    w_gates = w_in0[:, 2 * b:].astype(bf)
    inv_freq = (ROPE_THETA ** (-jnp.arange(ROPE_HALF, dtype=f32) * 2.0 / ROPE_DIM)).reshape(ROPE_HALF, 1)
    row = lambda v: v.reshape(1, -1)

    cparams = functools.partial(pltpu.CompilerParams, vmem_limit_bytes=VMEM_LIMIT_BYTES)
    tok = lambda i: (i, 0)
    head_t = pl.BlockSpec((N_HEADS, 1, HEAD_DIM, BLK), lambda i: (0, i, 0, 0))
    t_shape = (N_HEADS, nb, HEAD_DIM, BLK)

    h1, qa, ka, kmean, va, qb, kb, vb = pl.pallas_call(
        _ffn_proj_kernel,
        grid=(nb,),
        in_specs=[
            pl.BlockSpec((BLK, d), tok),
            pl.BlockSpec((1, BLK), lambda i: (0, i)),
            _resident((ROPE_HALF, 1)),
            _resident((1, d)),
            _resident((d, d_ff)), _resident((d, d_ff)), _resident((d_ff, d)),
            _resident((1, d)),
            _resident((5 * D_ATT, d)), _resident((d, D_ATT)),
        ],
        out_specs=[
            pl.BlockSpec((BLK, d), tok),
            head_t,
            pl.BlockSpec((BLK, D_ATT), tok),
            pl.BlockSpec((1, 1, D_ATT), lambda i: (i, 0, 0)),
            head_t, head_t,
            pl.BlockSpec((BLK, D_ATT), tok),
            head_t,
        ],
        out_shape=[
            jax.ShapeDtypeStruct((seq, d), f32),
            jax.ShapeDtypeStruct(t_shape, f32),
            jax.ShapeDtypeStruct((seq, D_ATT), bf),
            jax.ShapeDtypeStruct((nb, 1, D_ATT), f32),
            jax.ShapeDtypeStruct(t_shape, bf),
            jax.ShapeDtypeStruct(t_shape, bf),
            jax.ShapeDtypeStruct((seq, D_ATT), bf),
            jax.ShapeDtypeStruct(t_shape, bf),
        ],
        compiler_params=cparams(dimension_semantics=("arbitrary",)),
        name="ffn1_proj",
    )(x2, pos, inv_freq, row(ffn1_norm[0]), ffn1_w_gate[0].astype(bf), ffn1_w_up[0].astype(bf),
      ffn1_w_down[0].astype(bf), row(mix_norm[0]), w_t, w_kb)

    kmean_h = kmean.reshape(nb, N_HEADS, HEAD_DIM).transpose(1, 0, 2)

    k_all = _resident((seq, D_ATT))
    v_all = _resident(t_shape)
    y_spec = pl.BlockSpec((D_ATT, BLK), lambda i: (0, i))
    y_shape = jax.ShapeDtypeStruct((D_ATT, seq), bf)
    qz_scratch = pltpu.VMEM((N_HEADS, 2 * HEAD_DIM, BLK), bf)

    ya = pl.pallas_call(
        _moba_kernel,
        grid=(nb,),
        in_specs=[head_t, k_all, v_all, _resident((N_HEADS, nb, HEAD_DIM))],
        out_specs=y_spec,
        out_shape=y_shape,
        scratch_shapes=[qz_scratch,
                        pltpu.VMEM((N_HEADS, nb, BLK), f32),
                        pltpu.VMEM((N_HEADS, 1, BLK), f32),
                        pltpu.VMEM((N_HEADS, HEAD_DIM + BF16_ROWS, BLK), f32),
                        pltpu.VMEM((N_HEADS, GROUP * BLK, BLK), f32),
                        pltpu.VMEM((N_HEADS, GROUP * BLK, BLK), f32)],
        compiler_params=cparams(dimension_semantics=("arbitrary",)),
        name="moba_attention",
    )(qa, ka, va, kmean_h)

    yb = pl.pallas_call(
        _sb_kernel,
        grid=(nb,),
        in_specs=[head_t, k_all, v_all],
        out_specs=y_spec,
        out_shape=y_shape,
        scratch_shapes=[qz_scratch,
                        pltpu.VMEM((N_HEADS, 1, BLK), f32),
                        pltpu.VMEM((N_HEADS, HEAD_DIM, BLK), f32)],
        compiler_params=cparams(dimension_semantics=("arbitrary",)),
        name="stick_breaking_attention",
    )(qb, kb, vb)

    out = pl.pallas_call(
        _merge_ffn_kernel,
        grid=(nb,),
        in_specs=[
            pl.BlockSpec((BLK, d), tok),
            pl.BlockSpec((D_ATT, BLK), lambda i: (0, i)),
            pl.BlockSpec((D_ATT, BLK), lambda i: (0, i)),
            _resident((1, d)),
            _resident((d, 2 * d)), _resident((1, 2 * d)),
            _resident((D_ATT, d)), _resident((D_ATT, d)), _resident((d, d)),
            _resident((1, d)),
            _resident((d, d_ff)), _resident((d, d_ff)), _resident((d_ff, d)),
            _resident((1, d)),
        ],
        out_specs=pl.BlockSpec((BLK, d), tok),
        out_shape=jax.ShapeDtypeStruct((seq, d), f32),
        compiler_params=cparams(dimension_semantics=("arbitrary",)),
        name="merge_ffn2",
    )(h1, ya, yb, row(mix_norm[0]), w_gates, row(b_gate[0]), w_branch_moba[0].astype(bf),
      w_branch_sb[0].astype(bf), w_out[0].astype(bf), row(ffn2_norm[0]), ffn2_w_gate[0].astype(bf),
      ffn2_w_up[0].astype(bf), ffn2_w_down[0].astype(bf), row(final_norm))
    return out[None]
```

```python
import functools

import jax
import jax.numpy as jnp
from jax import lax
from jax.experimental import pallas as pl
from jax.experimental.pallas import tpu as pltpu

HEAD_DIM = 64
N_HEADS = 8
D_ATT = N_HEADS * HEAD_DIM
LANES = 128
BF16_ROWS = 16
BLK = 256
TOP_K = 3
GROUP = 2
ROPE_THETA = 500000.0
ROPE_DIM = HEAD_DIM // 4
ROPE_HALF = ROPE_DIM // 2
FFN_RES = 0.5
EPS = 1e-6
NEG = -1e30
SCALE = HEAD_DIM ** -0.5
LOG2E = 1.4426950408889634
SB_LOG_ZERO = -104.0
VMEM_LIMIT_BYTES = 56 * 1024 * 1024

_NT = (((1,), (1,)), ((), ()))
_TN = (((0,), (0,)), ((), ()))


def _rms(x, g):
    return x * lax.rsqrt(jnp.mean(x * x, axis=-1, keepdims=True) + EPS) * g


def _swiglu(xn, wg_ref, wu_ref, wd_ref):
    xb = xn.astype(jnp.bfloat16)
    g = jnp.dot(xb, wg_ref[...], preferred_element_type=jnp.float32)
    u = jnp.dot(xb, wu_ref[...], preferred_element_type=jnp.float32)
    act = (g * (1.0 / (1.0 + jnp.exp(-g))) * u).astype(jnp.bfloat16)
    return jnp.dot(act, wd_ref[...], preferred_element_type=jnp.float32)


def _ffn_proj_kernel(x_ref, pos_ref, invf_ref, g1_ref, wg_ref, wu_ref, wd_ref, gm_ref,
                     wt_ref, wkb_ref,
                     h_ref, qa_ref, ka_ref, kmean_ref, va_ref, qb_ref, kb_ref, vb_ref):
    x = x_ref[...]
    h = x + FFN_RES * _swiglu(_rms(x, g1_ref[...]), wg_ref, wu_ref, wd_ref)
    h_ref[...] = h
    u = _rms(h, gm_ref[...]).astype(jnp.bfloat16)
    tm = u.shape[0]
    pt = lax.dot_general(wt_ref[...], u, _NT, preferred_element_type=jnp.float32)
    kb = jnp.dot(u, wkb_ref[...], preferred_element_type=jnp.float32)

    ang = invf_ref[...] * pos_ref[...].astype(jnp.float32)
    cos = jnp.cos(ang)[None]
    sin = jnp.sin(ang)[None]

    def rotary(t):
        x1 = t[:, :ROPE_HALF]
        x2 = t[:, ROPE_HALF:ROPE_DIM]
        return jnp.concatenate([x1 * cos - x2 * sin, x2 * cos + x1 * sin, t[:, ROPE_DIM:]], axis=1)

    def heads(lo):
        return pt[lo * D_ATT:(lo + 1) * D_ATT].reshape(N_HEADS, HEAD_DIM, tm)

    qa_ref[:, 0] = rotary(heads(0))
    ka = rotary(heads(1)).reshape(D_ATT, tm).T
    kmean_ref[0] = jnp.mean(ka, axis=0, keepdims=True)
    ka_ref[...] = ka.astype(jnp.bfloat16)
    kb_ref[...] = kb.astype(jnp.bfloat16)
    va_ref[:, 0] = heads(2).astype(jnp.bfloat16)
    qb_ref[:, 0] = (heads(3) * SCALE).astype(jnp.bfloat16)
    vb_ref[:, 0] = heads(4).astype(jnp.bfloat16)


def _pad_pair(q, hh):
    z = jnp.zeros_like(q)
    return jnp.concatenate([q, z] if hh % 2 == 0 else [z, q], axis=0)


def _k_rows(k_ref, j, nblk, hh):
    p = hh // 2
    return k_ref[pl.ds(pl.multiple_of(j * BLK, BLK), nblk * BLK), p * LANES:(p + 1) * LANES]


def _moba_kernel(q_ref, k_ref, v_ref, kmean_ref, o_ref, qz_ref, bias_ref, m_ref, acc_ref,
                 sa_ref, sb_ref):
    i = pl.program_id(0)
    nb, tq = bias_ref.shape[1:]
    blk = lax.broadcasted_iota(jnp.int32, (nb, tq), 0)
    heads = range(N_HEADS)

    def v_aug(hh, j, nblk):
        ones = jnp.ones((BF16_ROWS, nblk * BLK), jnp.bfloat16)
        return jnp.concatenate(
            [jnp.concatenate([v_ref[hh, j + u] for u in range(nblk)], axis=1), ones], axis=0)

    q32 = [q_ref[hh, 0] for hh in heads]
    gates = [jnp.dot(kmean_ref[hh], q32[hh], preferred_element_type=jnp.float32,
                     precision=lax.Precision.HIGHEST) for hh in heads]
    for hh in heads:
        qz_ref[hh] = _pad_pair((q32[hh] * (SCALE * LOG2E)).astype(jnp.bfloat16), hh)
    for hh in heads:
        g = jnp.where(blk < i, gates[hh], NEG)
        sel = jnp.zeros((nb, tq), jnp.bool_)
        for _ in range(TOP_K):
            mx = jnp.max(g, axis=0, keepdims=True)
            first = jnp.min(jnp.where(g == mx, blk, nb), axis=0, keepdims=True)
            pick = blk == first
            sel = sel | pick
            g = jnp.where(pick, -jnp.inf, g)
        bias_ref[hh] = jnp.where(sel & (blk < i), 0.0, NEG)

    row = lax.broadcasted_iota(jnp.int32, (BLK, tq), 0)
    col = lax.broadcasted_iota(jnp.int32, (BLK, tq), 1)
    s_own = [jnp.dot(_k_rows(k_ref, i, 1, hh), qz_ref[hh], preferred_element_type=jnp.float32)
             for hh in heads]
    for hh in heads:
        s = jnp.where(row <= col, s_own[hh], NEG)
        m = jnp.max(s, axis=0, keepdims=True).astype(jnp.bfloat16).astype(jnp.float32)
        p = jnp.exp2(s - m)
        m_ref[hh] = m
        acc_ref[hh] = jnp.dot(v_aug(hh, i, 1), p.astype(jnp.bfloat16),
                              preferred_element_type=jnp.float32)

    n_groups_max = nb // GROUP

    def scores_into(dst_ref, grp, hh):
        dst_ref[hh] = jnp.dot(_k_rows(k_ref, grp * GROUP, GROUP, hh), qz_ref[hh],
                              preferred_element_type=jnp.float32).astype(jnp.bfloat16)

    def probs(src_ref, grp, hh):
        j = grp * GROUP
        parts = [src_ref[hh, u * BLK:(u + 1) * BLK] for u in range(GROUP)]
        biases = [bias_ref[hh, pl.ds(j + u, 1), :] for u in range(GROUP)]
        m_old = m_ref[hh]
        m_new = m_old
        for s_u, b_u in zip(parts, biases):
            m_new = jnp.maximum(m_new, jnp.max(s_u, axis=0, keepdims=True).astype(jnp.float32) + b_u)
        p = jnp.concatenate([jnp.exp2(s_u - (m_new - b_u).astype(jnp.bfloat16))
                             for s_u, b_u in zip(parts, biases)], axis=0)
        m_ref[hh] = m_new
        return hh, j, jnp.exp2(m_old - m_new), p

    def accumulate(hh, j, alpha, p):
        acc_ref[hh] = alpha * acc_ref[hh] + jnp.dot(v_aug(hh, j, GROUP), p,
                                                   preferred_element_type=jnp.float32)

    for hh in heads:
        scores_into(sa_ref, 0, hh)

    def two_groups(t, carry):
        g0 = 2 * t
        nxt = jnp.minimum(g0 + 2, n_groups_max - 1)
        pending = None
        for src_ref, dst_ref, grp, grp_next in ((sa_ref, sb_ref, g0, g0 + 1),
                                                (sb_ref, sa_ref, g0 + 1, nxt)):
            for hh in heads:
                scores_into(dst_ref, grp_next, hh)
                ready = probs(src_ref, grp, hh)
                if pending is not None:
                    accumulate(*pending)
                pending = ready
        accumulate(*pending)
        return carry

    n_groups = (i + GROUP - 1) // GROUP
    lax.fori_loop(0, (n_groups + 1) // 2, two_groups, 0)

    for hh in heads:
        acc = acc_ref[hh]
        o_ref[hh * HEAD_DIM:(hh + 1) * HEAD_DIM, :] = (
            acc[:HEAD_DIM] * (1.0 / acc[HEAD_DIM:HEAD_DIM + 1])).astype(o_ref.dtype)


def _sb_kernel(q_ref, k_ref, v_ref, o_ref, qz_ref, c_ref, acc_ref):
    i = pl.program_id(0)
    tq = q_ref.shape[-1]
    row = lax.broadcasted_iota(jnp.int32, (BLK, tq), 0)
    col = lax.broadcasted_iota(jnp.int32, (BLK, tq), 1)
    jj = lax.broadcasted_iota(jnp.int32, (BLK, 2 * BLK), 1) % BLK
    ss = lax.broadcasted_iota(jnp.int32, (BLK, 2 * BLK), 0)
    upper = jnp.where(jj > ss, 1.0, 0.0).astype(jnp.bfloat16)

    heads = range(N_HEADS)

    def block(j, diag):
        zs = [jnp.dot(_k_rows(k_ref, j, 1, hh), qz_ref[hh], preferred_element_type=jnp.float32)
              for hh in heads]
        log_sigs, log_fails, laters = [], [], []
        for hh in heads:
            z = zs[hh]
            log_sig = jnp.minimum(z, 0.0) - jnp.log(1.0 + jnp.exp(-jnp.abs(z)))
            log_fail = log_sig - z
            if diag:
                log_fail = jnp.where(row < col, log_fail, 0.0)
            hi = log_fail.astype(jnp.bfloat16)
            lo = (log_fail - hi.astype(jnp.float32)).astype(jnp.bfloat16)
            laters.append(jnp.dot(upper, jnp.concatenate([hi, lo], axis=0),
                                  preferred_element_type=jnp.float32))
            log_sigs.append(log_sig)
            log_fails.append(log_fail)
        cmax = None
        for hh in heads:
            c = c_ref[hh]
            w = jnp.exp(log_sigs[hh] + laters[hh] + c)
            if diag:
                w = jnp.where(row < col, w, 0.0)
            acc_ref[hh] += jnp.dot(v_ref[hh, j], w.astype(jnp.bfloat16),
                                   preferred_element_type=jnp.float32)
            c_new = c + jnp.sum(log_fails[hh], axis=0, keepdims=True)
            c_ref[hh] = c_new
            cmax = c_new if cmax is None else jnp.maximum(cmax, c_new)
        return jnp.max(cmax)

    for hh in heads:
        qz_ref[hh] = _pad_pair(q_ref[hh, 0], hh)
    c_ref[...] = jnp.zeros_like(c_ref)
    acc_ref[...] = jnp.zeros_like(acc_ref)
    cmax = block(i, True)

    def cond(state):
        j, cmax = state
        return (j >= 0) & (cmax > SB_LOG_ZERO)

    def body(state):
        j, _ = state
        return j - 1, block(j, False)

    lax.while_loop(cond, body, (i - 1, cmax))
    for hh in range(N_HEADS):
        o_ref[hh * HEAD_DIM:(hh + 1) * HEAD_DIM, :] = acc_ref[hh].astype(o_ref.dtype)


def _merge_ffn_kernel(h_ref, ya_ref, yb_ref, gm_ref, wgate_ref, bg_ref, wa_ref, wb_ref, wo_ref,
                      g2_ref, wg_ref, wu_ref, wd_ref, gf_ref, o_ref):
    h = h_ref[...]
    d = h.shape[-1]
    u = _rms(h, gm_ref[...]).astype(jnp.bfloat16)
    gl = jnp.dot(u, wgate_ref[...], preferred_element_type=jnp.float32) + bg_ref[...]
    gates = 1.0 / (1.0 + jnp.exp(-gl))
    ma = lax.dot_general(ya_ref[...], wa_ref[...], _TN, preferred_element_type=jnp.float32)
    mb = lax.dot_general(yb_ref[...], wb_ref[...], _TN, preferred_element_type=jnp.float32)
    merged = gates[:, :d] * ma + gates[:, d:] * mb
    h = h + jnp.dot(merged.astype(jnp.bfloat16), wo_ref[...], preferred_element_type=jnp.float32)
    h = h + FFN_RES * _swiglu(_rms(h, g2_ref[...]), wg_ref, wu_ref, wd_ref)
    o_ref[...] = _rms(h, gf_ref[...])


def _resident(shape):
    return pl.BlockSpec(shape, lambda *_: (0,) * len(shape), pipeline_mode=pl.Buffered(1))


def kernel(x, positions, ffn1_norm, ffn1_w_gate, ffn1_w_up, ffn1_w_down, mix_norm, w_in, b_gate,
           w_branch_moba, w_branch_sb, w_out, ffn2_norm, ffn2_w_gate, ffn2_w_up, ffn2_w_down,
           final_norm):
    batch, seq, d = x.shape
    assert batch == 1 and ffn1_norm.shape[0] == 1 and seq % BLK == 0 and d == w_out.shape[-1]
    nb = seq // BLK
    d_ff = ffn1_w_gate.shape[-1]
    bf = jnp.bfloat16
    f32 = jnp.float32
    x2 = x[0]
    pos = positions.reshape(1, seq)
    w_in0 = w_in[0]
    a, b = D_ATT, 3 * D_ATT
    w_t = jnp.concatenate([w_in0[:, :b], w_in0[:, b:b + a], w_in0[:, b + 2 * a:b + 3 * a]], axis=1).T.astype(bf)
    w_kb = w_in0[:, b + a:b + 2 * a].astype(bf)
    w_gates = w_in0[:, 2 * b:].astype(bf)
    inv_freq = (ROPE_THETA ** (-jnp.arange(ROPE_HALF, dtype=f32) * 2.0 / ROPE_DIM)).reshape(ROPE_HALF, 1)
    row = lambda v: v.reshape(1, -1)

    cparams = functools.partial(pltpu.CompilerParams, vmem_limit_bytes=VMEM_LIMIT_BYTES)
    tok = lambda i: (i, 0)
    head_t = pl.BlockSpec((N_HEADS, 1, HEAD_DIM, BLK), lambda i: (0, i, 0, 0))
    t_shape = (N_HEADS, nb, HEAD_DIM, BLK)

    h1, qa, ka, kmean, va, qb, kb, vb = pl.pallas_call(
        _ffn_proj_kernel,
        grid=(nb,),
        in_specs=[
            pl.BlockSpec((BLK, d), tok),
            pl.BlockSpec((1, BLK), lambda i: (0, i)),
            _resident((ROPE_HALF, 1)),
            _resident((1, d)),
            _resident((d, d_ff)), _resident((d, d_ff)), _resident((d_ff, d)),
            _resident((1, d)),
            _resident((5 * D_ATT, d)), _resident((d, D_ATT)),
        ],
        out_specs=[
            pl.BlockSpec((BLK, d), tok),
            head_t,
            pl.BlockSpec((BLK, D_ATT), tok),
            pl.BlockSpec((1, 1, D_ATT), lambda i: (i, 0, 0)),
            head_t, head_t,
            pl.BlockSpec((BLK, D_ATT), tok),
            head_t,
        ],
        out_shape=[
            jax.ShapeDtypeStruct((seq, d), f32),
            jax.ShapeDtypeStruct(t_shape, f32),
            jax.ShapeDtypeStruct((seq, D_ATT), bf),
            jax.ShapeDtypeStruct((nb, 1, D_ATT), f32),
            jax.ShapeDtypeStruct(t_shape, bf),
            jax.ShapeDtypeStruct(t_shape, bf),
            jax.ShapeDtypeStruct((seq, D_ATT), bf),
            jax.ShapeDtypeStruct(t_shape, bf),
        ],
        compiler_params=cparams(dimension_semantics=("arbitrary",)),
        name="ffn1_proj",
    )(x2, pos, inv_freq, row(ffn1_norm[0]), ffn1_w_gate[0].astype(bf), ffn1_w_up[0].astype(bf),
      ffn1_w_down[0].astype(bf), row(mix_norm[0]), w_t, w_kb)

    kmean_h = kmean.reshape(nb, N_HEADS, HEAD_DIM).transpose(1, 0, 2)

    k_all = _resident((seq, D_ATT))
    v_all = _resident(t_shape)
    y_spec = pl.BlockSpec((D_ATT, BLK), lambda i: (0, i))
    y_shape = jax.ShapeDtypeStruct((D_ATT, seq), bf)
    qz_scratch = pltpu.VMEM((N_HEADS, 2 * HEAD_DIM, BLK), bf)

    ya = pl.pallas_call(
        _moba_kernel,
        grid=(nb,),
        in_specs=[head_t, k_all, v_all, _resident((N_HEADS, nb, HEAD_DIM))],
        out_specs=y_spec,
        out_shape=y_shape,
        scratch_shapes=[qz_scratch,
                        pltpu.VMEM((N_HEADS, nb, BLK), f32),
                        pltpu.VMEM((N_HEADS, 1, BLK), f32),
                        pltpu.VMEM((N_HEADS, HEAD_DIM + BF16_ROWS, BLK), f32),
                        pltpu.VMEM((N_HEADS, GROUP * BLK, BLK), bf),
                        pltpu.VMEM((N_HEADS, GROUP * BLK, BLK), bf)],
        compiler_params=cparams(dimension_semantics=("arbitrary",)),
        name="moba_attention",
    )(qa, ka, va, kmean_h)

    yb = pl.pallas_call(
        _sb_kernel,
        grid=(nb,),
        in_specs=[head_t, k_all, v_all],
        out_specs=y_spec,
        out_shape=y_shape,
        scratch_shapes=[qz_scratch,
                        pltpu.VMEM((N_HEADS, 1, BLK), f32),
                        pltpu.VMEM((N_HEADS, HEAD_DIM, BLK), f32)],
        compiler_params=cparams(dimension_semantics=("arbitrary",)),
        name="stick_breaking_attention",
    )(qb, kb, vb)

    out = pl.pallas_call(
        _merge_ffn_kernel,
        grid=(nb,),
        in_specs=[
            pl.BlockSpec((BLK, d), tok),
            pl.BlockSpec((D_ATT, BLK), lambda i: (0, i)),
            pl.BlockSpec((D_ATT, BLK), lambda i: (0, i)),
            _resident((1, d)),
            _resident((d, 2 * d)), _resident((1, 2 * d)),
            _resident((D_ATT, d)), _resident((D_ATT, d)), _resident((d, d)),
            _resident((1, d)),
            _resident((d, d_ff)), _resident((d, d_ff)), _resident((d_ff, d)),
            _resident((1, d)),
        ],
        out_specs=pl.BlockSpec((BLK, d), tok),
        out_shape=jax.ShapeDtypeStruct((seq, d), f32),
        compiler_params=cparams(dimension_semantics=("arbitrary",)),
        name="merge_ffn2",
    )(h1, ya, yb, row(mix_norm[0]), w_gates, row(b_gate[0]), w_branch_moba[0].astype(bf),
      w_branch_sb[0].astype(bf), w_out[0].astype(bf), row(ffn2_norm[0]), ffn2_w_gate[0].astype(bf),
      ffn2_w_up[0].astype(bf), ffn2_w_down[0].astype(bf), row(final_norm))
    return out[None]
```

```python
import functools

import jax
import jax.numpy as jnp
from jax import lax
from jax.experimental import pallas as pl
from jax.experimental.pallas import tpu as pltpu

HEAD_DIM = 64
N_HEADS = 8
D_ATT = N_HEADS * HEAD_DIM
LANES = 128
BF16_ROWS = 16
BLK = 256
TOP_K = 3
GROUP = 2
ROPE_THETA = 500000.0
ROPE_DIM = HEAD_DIM // 4
ROPE_HALF = ROPE_DIM // 2
FFN_RES = 0.5
EPS = 1e-6
NEG = -1e30
SCALE = HEAD_DIM ** -0.5
LOG2E = 1.4426950408889634
SB_LOG_ZERO = -104.0
VMEM_LIMIT_BYTES = 56 * 1024 * 1024

_NT = (((1,), (1,)), ((), ()))
_TN = (((0,), (0,)), ((), ()))


def _rms(x, g):
    return x * lax.rsqrt(jnp.mean(x * x, axis=-1, keepdims=True) + EPS) * g


def _swiglu(xn, wg_ref, wu_ref, wd_ref):
    xb = xn.astype(jnp.bfloat16)
    g = jnp.dot(xb, wg_ref[...], preferred_element_type=jnp.float32)
    u = jnp.dot(xb, wu_ref[...], preferred_element_type=jnp.float32)
    act = (g * (1.0 / (1.0 + jnp.exp(-g))) * u).astype(jnp.bfloat16)
    return jnp.dot(act, wd_ref[...], preferred_element_type=jnp.float32)


def _ffn_proj_kernel(x_ref, pos_ref, invf_ref, g1_ref, wg_ref, wu_ref, wd_ref, gm_ref,
                     wt_ref, wkb_ref,
                     h_ref, qa_ref, ka_ref, kmean_ref, va_ref, qb_ref, kb_ref, vb_ref):
    x = x_ref[...]
    h = x + FFN_RES * _swiglu(_rms(x, g1_ref[...]), wg_ref, wu_ref, wd_ref)
    h_ref[...] = h
    u = _rms(h, gm_ref[...]).astype(jnp.bfloat16)
    tm = u.shape[0]
    pt = lax.dot_general(wt_ref[...], u, _NT, preferred_element_type=jnp.float32)
    kb = jnp.dot(u, wkb_ref[...], preferred_element_type=jnp.float32)

    ang = invf_ref[...] * pos_ref[...].astype(jnp.float32)
    cos = jnp.cos(ang)[None]
    sin = jnp.sin(ang)[None]

    def rotary(t):
        x1 = t[:, :ROPE_HALF]
        x2 = t[:, ROPE_HALF:ROPE_DIM]
        return jnp.concatenate([x1 * cos - x2 * sin, x2 * cos + x1 * sin, t[:, ROPE_DIM:]], axis=1)

    def heads(lo):
        return pt[lo * D_ATT:(lo + 1) * D_ATT].reshape(N_HEADS, HEAD_DIM, tm)

    qa_ref[:, 0] = rotary(heads(0))
    ka = rotary(heads(1)).reshape(D_ATT, tm).T
    kmean_ref[0] = jnp.mean(ka, axis=0, keepdims=True)
    ka_ref[...] = ka.astype(jnp.bfloat16)
    kb_ref[...] = kb.astype(jnp.bfloat16)
    va_ref[:, 0] = heads(2).astype(jnp.bfloat16)
    qb_ref[:, 0] = (heads(3) * SCALE).astype(jnp.bfloat16)
    vb_ref[:, 0] = heads(4).astype(jnp.bfloat16)


def _pad_pair(q, hh):
    z = jnp.zeros_like(q)
    return jnp.concatenate([q, z] if hh % 2 == 0 else [z, q], axis=0)


def _k_rows(k_ref, j, nblk, hh):
    p = hh // 2
    return k_ref[pl.ds(pl.multiple_of(j * BLK, BLK), nblk * BLK), p * LANES:(p + 1) * LANES]


def _moba_kernel(q_ref, k_ref, v_ref, kmean_ref, o_ref, qz_ref, bias_ref, m_ref, acc_ref,
                 sa_ref, sb_ref):
    i = pl.program_id(0)
    nb, tq = bias_ref.shape[1:]
    blk = lax.broadcasted_iota(jnp.int32, (nb, tq), 0)
    heads = range(N_HEADS)

    def v_aug(hh, j, nblk):
        ones = jnp.ones((BF16_ROWS, nblk * BLK), jnp.bfloat16)
        return jnp.concatenate(
            [jnp.concatenate([v_ref[hh, j + u] for u in range(nblk)], axis=1), ones], axis=0)

    q32 = [q_ref[hh, 0] for hh in heads]
    gates = [jnp.dot(kmean_ref[hh], q32[hh], preferred_element_type=jnp.float32,
                     precision=lax.Precision.HIGHEST) for hh in heads]
    for hh in heads:
        qz_ref[hh] = _pad_pair((q32[hh] * (SCALE * LOG2E)).astype(jnp.bfloat16), hh)
    for hh in heads:
        g = jnp.where(blk < i, gates[hh], NEG)
        sel = jnp.zeros((nb, tq), jnp.bool_)
        for _ in range(TOP_K):
            mx = jnp.max(g, axis=0, keepdims=True)
            first = jnp.min(jnp.where(g == mx, blk, nb), axis=0, keepdims=True)
            pick = blk == first
            sel = sel | pick
            g = jnp.where(pick, -jnp.inf, g)
        bias_ref[hh] = jnp.where(sel & (blk < i), 0.0, NEG)

    row = lax.broadcasted_iota(jnp.int32, (BLK, tq), 0)
    col = lax.broadcasted_iota(jnp.int32, (BLK, tq), 1)
    s_own = [jnp.dot(_k_rows(k_ref, i, 1, hh), qz_ref[hh], preferred_element_type=jnp.float32)
             for hh in heads]
    for hh in heads:
        s = jnp.where(row <= col, s_own[hh], NEG)
        m = jnp.max(s, axis=0, keepdims=True).astype(jnp.bfloat16).astype(jnp.float32)
        p = jnp.exp2(s - m)
        m_ref[hh] = m
        acc_ref[hh] = jnp.dot(v_aug(hh, i, 1), p.astype(jnp.bfloat16),
                              preferred_element_type=jnp.float32)

    n_groups_max = nb // GROUP

    def scores_into(dst_ref, grp, hh):
        dst_ref[hh] = jnp.dot(_k_rows(k_ref, grp * GROUP, GROUP, hh), qz_ref[hh],
                              preferred_element_type=jnp.float32).astype(jnp.bfloat16)

    def probs(src_ref, grp, hh):
        j = grp * GROUP
        parts = [src_ref[hh, u * BLK:(u + 1) * BLK] for u in range(GROUP)]
        biases = [bias_ref[hh, pl.ds(j + u, 1), :] for u in range(GROUP)]
        m_old = m_ref[hh]
        m_new = m_old
        for s_u, b_u in zip(parts, biases):
            m_new = jnp.maximum(m_new, jnp.max(s_u, axis=0, keepdims=True).astype(jnp.float32) + b_u)
        p = jnp.concatenate([jnp.exp2((s_u - (m_new - b_u).astype(jnp.bfloat16)).astype(jnp.float32))
                             for s_u, b_u in zip(parts, biases)], axis=0).astype(jnp.bfloat16)
        m_ref[hh] = m_new
        return hh, j, jnp.exp2(m_old - m_new), p

    def accumulate(hh, j, alpha, p):
        acc_ref[hh] = alpha * acc_ref[hh] + jnp.dot(v_aug(hh, j, GROUP), p,
                                                   preferred_element_type=jnp.float32)

    for hh in heads:
        scores_into(sa_ref, 0, hh)

    def two_groups(t, carry):
        g0 = 2 * t
        nxt = jnp.minimum(g0 + 2, n_groups_max - 1)
        pending = None
        for src_ref, dst_ref, grp, grp_next in ((sa_ref, sb_ref, g0, g0 + 1),
                                                (sb_ref, sa_ref, g0 + 1, nxt)):
            for hh in heads:
                scores_into(dst_ref, grp_next, hh)
                ready = probs(src_ref, grp, hh)
                if pending is not None:
                    accumulate(*pending)
                pending = ready
        accumulate(*pending)
        return carry

    n_groups = (i + GROUP - 1) // GROUP
    lax.fori_loop(0, (n_groups + 1) // 2, two_groups, 0)

    for hh in heads:
        acc = acc_ref[hh]
        o_ref[hh * HEAD_DIM:(hh + 1) * HEAD_DIM, :] = (
            acc[:HEAD_DIM] * (1.0 / acc[HEAD_DIM:HEAD_DIM + 1])).astype(o_ref.dtype)


def _sb_kernel(q_ref, k_ref, v_ref, o_ref, qz_ref, c_ref, acc_ref):
    i = pl.program_id(0)
    tq = q_ref.shape[-1]
    row = lax.broadcasted_iota(jnp.int32, (BLK, tq), 0)
    col = lax.broadcasted_iota(jnp.int32, (BLK, tq), 1)
    jj = lax.broadcasted_iota(jnp.int32, (BLK, 2 * BLK), 1) % BLK
    ss = lax.broadcasted_iota(jnp.int32, (BLK, 2 * BLK), 0)
    upper = jnp.where(jj > ss, 1.0, 0.0).astype(jnp.bfloat16)

    heads = range(N_HEADS)

    def block(j, diag):
        zs = [jnp.dot(_k_rows(k_ref, j, 1, hh), qz_ref[hh], preferred_element_type=jnp.float32)
              for hh in heads]
        log_sigs, log_fails, laters = [], [], []
        for hh in heads:
            z = zs[hh]
            log_sig = jnp.minimum(z, 0.0) - jnp.log(1.0 + jnp.exp(-jnp.abs(z)))
            log_fail = log_sig - z
            if diag:
                log_fail = jnp.where(row < col, log_fail, 0.0)
            hi = log_fail.astype(jnp.bfloat16)
            lo = (log_fail - hi.astype(jnp.float32)).astype(jnp.bfloat16)
            laters.append(jnp.dot(upper, jnp.concatenate([hi, lo], axis=0),
                                  preferred_element_type=jnp.float32))
            log_sigs.append(log_sig)
            log_fails.append(log_fail)
        cmax = None
        for hh in heads:
            c = c_ref[hh]
            w = jnp.exp(log_sigs[hh] + laters[hh] + c)
            if diag:
                w = jnp.where(row < col, w, 0.0)
            acc_ref[hh] += jnp.dot(v_ref[hh, j], w.astype(jnp.bfloat16),
                                   preferred_element_type=jnp.float32)
            c_new = c + jnp.sum(log_fails[hh], axis=0, keepdims=True)
            c_ref[hh] = c_new
            cmax = c_new if cmax is None else jnp.maximum(cmax, c_new)
        return jnp.max(cmax)

    for hh in heads:
        qz_ref[hh] = _pad_pair(q_ref[hh, 0], hh)
    c_ref[...] = jnp.zeros_like(c_ref)
    acc_ref[...] = jnp.zeros_like(acc_ref)
    cmax = block(i, True)

    def cond(state):
        j, cmax = state
        return (j >= 0) & (cmax > SB_LOG_ZERO)

    def body(state):
        j, _ = state
        return j - 1, block(j, False)

    lax.while_loop(cond, body, (i - 1, cmax))
    for hh in range(N_HEADS):
        o_ref[hh * HEAD_DIM:(hh + 1) * HEAD_DIM, :] = acc_ref[hh].astype(o_ref.dtype)


def _merge_ffn_kernel(h_ref, ya_ref, yb_ref, gm_ref, wgate_ref, bg_ref, wa_ref, wb_ref, wo_ref,
                      g2_ref, wg_ref, wu_ref, wd_ref, gf_ref, o_ref):
    h = h_ref[...]
    d = h.shape[-1]
    u = _rms(h, gm_ref[...]).astype(jnp.bfloat16)
    gl = jnp.dot(u, wgate_ref[...], preferred_element_type=jnp.float32) + bg_ref[...]
    gates = 1.0 / (1.0 + jnp.exp(-gl))
    ma = lax.dot_general(ya_ref[...], wa_ref[...], _TN, preferred_element_type=jnp.float32)
    mb = lax.dot_general(yb_ref[...], wb_ref[...], _TN, preferred_element_type=jnp.float32)
    merged = gates[:, :d] * ma + gates[:, d:] * mb
    h = h + jnp.dot(merged.astype(jnp.bfloat16), wo_ref[...], preferred_element_type=jnp.float32)
    h = h + FFN_RES * _swiglu(_rms(h, g2_ref[...]), wg_ref, wu_ref, wd_ref)
    o_ref[...] = _rms(h, gf_ref[...])


def _resident(shape):
    return pl.BlockSpec(shape, lambda *_: (0,) * len(shape), pipeline_mode=pl.Buffered(1))


def kernel(x, positions, ffn1_norm, ffn1_w_gate, ffn1_w_up, ffn1_w_down, mix_norm, w_in, b_gate,
           w_branch_moba, w_branch_sb, w_out, ffn2_norm, ffn2_w_gate, ffn2_w_up, ffn2_w_down,
           final_norm):
    batch, seq, d = x.shape
    assert batch == 1 and ffn1_norm.shape[0] == 1 and seq % BLK == 0 and d == w_out.shape[-1]
    nb = seq // BLK
    d_ff = ffn1_w_gate.shape[-1]
    bf = jnp.bfloat16
    f32 = jnp.float32
    x2 = x[0]
    pos = positions.reshape(1, seq)
    w_in0 = w_in[0]
    a, b = D_ATT, 3 * D_ATT
    w_t = jnp.concatenate([w_in0[:, :b], w_in0[:, b:b + a], w_in0[:, b + 2 * a:b + 3 * a]], axis=1).T.astype(bf)
    w_kb = w_in0[:, b + a:b + 2 * a].astype(bf)
    w_gates = w_in0[:, 2 * b:].astype(bf)
    inv_freq = (ROPE_THETA ** (-jnp.arange(ROPE_HALF, dtype=f32) * 2.0 / ROPE_DIM)).reshape(ROPE_HALF, 1)
    row = lambda v: v.reshape(1, -1)

    cparams = functools.partial(pltpu.CompilerParams, vmem_limit_bytes=VMEM_LIMIT_BYTES)
    tok = lambda i: (i, 0)
    head_t = pl.BlockSpec((N_HEADS, 1, HEAD_DIM, BLK), lambda i: (0, i, 0, 0))
    t_shape = (N_HEADS, nb, HEAD_DIM, BLK)

    h1, qa, ka, kmean, va, qb, kb, vb = pl.pallas_call(
        _ffn_proj_kernel,
        grid=(nb,),
        in_specs=[
            pl.BlockSpec((BLK, d), tok),
            pl.BlockSpec((1, BLK), lambda i: (0, i)),
            _resident((ROPE_HALF, 1)),
            _resident((1, d)),
            _resident((d, d_ff)), _resident((d, d_ff)), _resident((d_ff, d)),
            _resident((1, d)),
            _resident((5 * D_ATT, d)), _resident((d, D_ATT)),
        ],
        out_specs=[
            pl.BlockSpec((BLK, d), tok),
            head_t,
            pl.BlockSpec((BLK, D_ATT), tok),
            pl.BlockSpec((1, 1, D_ATT), lambda i: (i, 0, 0)),
            head_t, head_t,
            pl.BlockSpec((BLK, D_ATT), tok),
            head_t,
        ],
        out_shape=[
            jax.ShapeDtypeStruct((seq, d), f32),
            jax.ShapeDtypeStruct(t_shape, f32),
            jax.ShapeDtypeStruct((seq, D_ATT), bf),
            jax.ShapeDtypeStruct((nb, 1, D_ATT), f32),
            jax.ShapeDtypeStruct(t_shape, bf),
            jax.ShapeDtypeStruct(t_shape, bf),
            jax.ShapeDtypeStruct((seq, D_ATT), bf),
            jax.ShapeDtypeStruct(t_shape, bf),
        ],
        compiler_params=cparams(dimension_semantics=("arbitrary",)),
        name="ffn1_proj",
    )(x2, pos, inv_freq, row(ffn1_norm[0]), ffn1_w_gate[0].astype(bf), ffn1_w_up[0].astype(bf),
      ffn1_w_down[0].astype(bf), row(mix_norm[0]), w_t, w_kb)

    kmean_h = kmean.reshape(nb, N_HEADS, HEAD_DIM).transpose(1, 0, 2)

    k_all = _resident((seq, D_ATT))
    v_all = _resident(t_shape)
    y_spec = pl.BlockSpec((D_ATT, BLK), lambda i: (0, i))
    y_shape = jax.ShapeDtypeStruct((D_ATT, seq), bf)
    qz_scratch = pltpu.VMEM((N_HEADS, 2 * HEAD_DIM, BLK), bf)

    ya = pl.pallas_call(
        _moba_kernel,
        grid=(nb,),
        in_specs=[head_t, k_all, v_all, _resident((N_HEADS, nb, HEAD_DIM))],
        out_specs=y_spec,
        out_shape=y_shape,
        scratch_shapes=[qz_scratch,
                        pltpu.VMEM((N_HEADS, nb, BLK), f32),
                        pltpu.VMEM((N_HEADS, 1, BLK), f32),
                        pltpu.VMEM((N_HEADS, HEAD_DIM + BF16_ROWS, BLK), f32),
                        pltpu.VMEM((N_HEADS, GROUP * BLK, BLK), bf),
                        pltpu.VMEM((N_HEADS, GROUP * BLK, BLK), bf)],
        compiler_params=cparams(dimension_semantics=("arbitrary",)),
        name="moba_attention",
    )(qa, ka, va, kmean_h)

    yb = pl.pallas_call(
        _sb_kernel,
        grid=(nb,),
        in_specs=[head_t, k_all, v_all],
        out_specs=y_spec,
        out_shape=y_shape,
        scratch_shapes=[qz_scratch,
                        pltpu.VMEM((N_HEADS, 1, BLK), f32),
                        pltpu.VMEM((N_HEADS, HEAD_DIM, BLK), f32)],
        compiler_params=cparams(dimension_semantics=("arbitrary",)),
        name="stick_breaking_attention",
    )(qb, kb, vb)

    out = pl.pallas_call(
        _merge_ffn_kernel,
        grid=(nb,),
        in_specs=[
            pl.BlockSpec((BLK, d), tok),
            pl.BlockSpec((D_ATT, BLK), lambda i: (0, i)),
            pl.BlockSpec((D_ATT, BLK), lambda i: (0, i)),
            _resident((1, d)),
            _resident((d, 2 * d)), _resident((1, 2 * d)),
            _resident((D_ATT, d)), _resident((D_ATT, d)), _resident((d, d)),
            _resident((1, d)),
            _resident((d, d_ff)), _resident((d, d_ff)), _resident((d_ff, d)),
            _resident((1, d)),
        ],
        out_specs=pl.BlockSpec((BLK, d), tok),
        out_shape=jax.ShapeDtypeStruct((seq, d), f32),
        compiler_params=cparams(dimension_semantics=("arbitrary",)),
        name="merge_ffn2",
    )(h1, ya, yb, row(mix_norm[0]), w_gates, row(b_gate[0]), w_branch_moba[0].astype(bf),
      w_branch_sb[0].astype(bf), w_out[0].astype(bf), row(ffn2_norm[0]), ffn2_w_gate[0].astype(bf),
      ffn2_w_up[0].astype(bf), ffn2_w_down[0].astype(bf), row(final_norm))
    return out[None]
```

```python
import functools

import jax
import jax.numpy as jnp
from jax import lax
from jax.experimental import pallas as pl
from jax.experimental.pallas import tpu as pltpu

HEAD_DIM = 64
N_HEADS = 8
D_ATT = N_HEADS * HEAD_DIM
LANES = 128
BF16_ROWS = 16
BLK = 256
TOP_K = 3
GROUP = 2
ROPE_THETA = 500000.0
ROPE_DIM = HEAD_DIM // 4
ROPE_HALF = ROPE_DIM // 2
FFN_RES = 0.5
EPS = 1e-6
NEG = -1e30
SCALE = HEAD_DIM ** -0.5
LOG2E = 1.4426950408889634
SB_LOG_ZERO = -104.0
VMEM_LIMIT_BYTES = 56 * 1024 * 1024

_NT = (((1,), (1,)), ((), ()))
_TN = (((0,), (0,)), ((), ()))


def _rms(x, g):
    return x * lax.rsqrt(jnp.mean(x * x, axis=-1, keepdims=True) + EPS) * g


def _swiglu(xn, wg_ref, wu_ref, wd_ref):
    xb = xn.astype(jnp.bfloat16)
    g = jnp.dot(xb, wg_ref[...], preferred_element_type=jnp.float32)
    u = jnp.dot(xb, wu_ref[...], preferred_element_type=jnp.float32)
    act = (g * (1.0 / (1.0 + jnp.exp(-g))) * u).astype(jnp.bfloat16)
    return jnp.dot(act, wd_ref[...], preferred_element_type=jnp.float32)


def _ffn_proj_kernel(x_ref, pos_ref, invf_ref, g1_ref, wg_ref, wu_ref, wd_ref, gm_ref,
                     wt_ref, wkb_ref,
                     h_ref, qa_ref, ka_ref, kmean_ref, va_ref, qb_ref, kb_ref, vb_ref):
    x = x_ref[...]
    h = x + FFN_RES * _swiglu(_rms(x, g1_ref[...]), wg_ref, wu_ref, wd_ref)
    h_ref[...] = h
    u = _rms(h, gm_ref[...]).astype(jnp.bfloat16)
    tm = u.shape[0]
    pt = lax.dot_general(wt_ref[...], u, _NT, preferred_element_type=jnp.float32)
    kb = jnp.dot(u, wkb_ref[...], preferred_element_type=jnp.float32)

    ang = invf_ref[...] * pos_ref[...].astype(jnp.float32)
    cos = jnp.cos(ang)[None]
    sin = jnp.sin(ang)[None]

    def rotary(t):
        x1 = t[:, :ROPE_HALF]
        x2 = t[:, ROPE_HALF:ROPE_DIM]
        return jnp.concatenate([x1 * cos - x2 * sin, x2 * cos + x1 * sin, t[:, ROPE_DIM:]], axis=1)

    def heads(lo):
        return pt[lo * D_ATT:(lo + 1) * D_ATT].reshape(N_HEADS, HEAD_DIM, tm)

    qa_ref[:, 0] = rotary(heads(0))
    ka = rotary(heads(1)).reshape(D_ATT, tm).T
    kmean_ref[0] = jnp.mean(ka, axis=0, keepdims=True)
    ka_ref[...] = ka.astype(jnp.bfloat16)
    kb_ref[...] = kb.astype(jnp.bfloat16)
    va_ref[:, 0] = heads(2).astype(jnp.bfloat16)
    qb_ref[:, 0] = (heads(3) * SCALE).astype(jnp.bfloat16)
    vb_ref[:, 0] = heads(4).astype(jnp.bfloat16)


def _pad_pair(q, hh):
    z = jnp.zeros_like(q)
    return jnp.concatenate([q, z] if hh % 2 == 0 else [z, q], axis=0)


def _k_rows(k_ref, j, nblk, hh):
    p = hh // 2
    return k_ref[pl.ds(pl.multiple_of(j * BLK, BLK), nblk * BLK), p * LANES:(p + 1) * LANES]


def _moba_kernel(q_ref, k_ref, v_ref, kmean_ref, o_ref, qz_ref, bias_ref, m_ref, acc_ref,
                 sa_ref, sb_ref):
    i = pl.program_id(0)
    nb, tq = bias_ref.shape[1:]
    blk = lax.broadcasted_iota(jnp.int32, (nb, tq), 0)
    heads = range(N_HEADS)

    def v_aug(hh, j, nblk):
        ones = jnp.ones((BF16_ROWS, nblk * BLK), jnp.bfloat16)
        return jnp.concatenate(
            [jnp.concatenate([v_ref[hh, j + u] for u in range(nblk)], axis=1), ones], axis=0)

    q32 = [q_ref[hh, 0] for hh in heads]
    gates = [jnp.dot(kmean_ref[hh], q32[hh], preferred_element_type=jnp.float32,
                     precision=lax.Precision.HIGHEST) for hh in heads]
    for hh in heads:
        qz_ref[hh] = _pad_pair((q32[hh] * (SCALE * LOG2E)).astype(jnp.bfloat16), hh)
    for hh in heads:
        g = jnp.where(blk < i, gates[hh], NEG)
        sel = jnp.zeros((nb, tq), jnp.bool_)
        for _ in range(TOP_K):
            mx = jnp.max(g, axis=0, keepdims=True)
            first = jnp.min(jnp.where(g == mx, blk, nb), axis=0, keepdims=True)
            pick = blk == first
            sel = sel | pick
            g = jnp.where(pick, -jnp.inf, g)
        bias_ref[hh] = jnp.where(sel & (blk < i), 0.0, NEG)

    row = lax.broadcasted_iota(jnp.int32, (BLK, tq), 0)
    col = lax.broadcasted_iota(jnp.int32, (BLK, tq), 1)
    s_own = [jnp.dot(_k_rows(k_ref, i, 1, hh), qz_ref[hh], preferred_element_type=jnp.float32)
             for hh in heads]
    for hh in heads:
        s = jnp.where(row <= col, s_own[hh], NEG)
        m = jnp.max(s, axis=0, keepdims=True)
        p = jnp.exp2(s - m)
        m_ref[hh] = m
        acc_ref[hh] = jnp.dot(v_aug(hh, i, 1), p.astype(jnp.bfloat16),
                              preferred_element_type=jnp.float32)

    n_groups_max = nb // GROUP

    def scores_into(dst_ref, grp, hh):
        dst_ref[hh] = jnp.dot(_k_rows(k_ref, grp * GROUP, GROUP, hh), qz_ref[hh],
                              preferred_element_type=jnp.float32)

    def probs(src_ref, grp, hh):
        j = grp * GROUP
        parts = [src_ref[hh, u * BLK:(u + 1) * BLK] for u in range(GROUP)]
        biases = [bias_ref[hh, pl.ds(j + u, 1), :] for u in range(GROUP)]
        m_old = m_ref[hh]
        m_new = m_old
        for s_u, b_u in zip(parts, biases):
            m_new = jnp.maximum(m_new, jnp.max(s_u, axis=0, keepdims=True) + b_u)
        p = jnp.concatenate([jnp.exp2((s_u - (m_new - b_u)).astype(jnp.bfloat16))
                             for s_u, b_u in zip(parts, biases)], axis=0)
        m_ref[hh] = m_new
        return hh, j, jnp.exp2(m_old - m_new), p

    def accumulate(hh, j, alpha, p):
        acc_ref[hh] = alpha * acc_ref[hh] + jnp.dot(v_aug(hh, j, GROUP), p,
                                                   preferred_element_type=jnp.float32)

    for hh in heads:
        scores_into(sa_ref, 0, hh)

    def two_groups(t, carry):
        g0 = 2 * t
        nxt = jnp.minimum(g0 + 2, n_groups_max - 1)
        pending = None
        for src_ref, dst_ref, grp, grp_next in ((sa_ref, sb_ref, g0, g0 + 1),
                                                (sb_ref, sa_ref, g0 + 1, nxt)):
            for hh in heads:
                scores_into(dst_ref, grp_next, hh)
                ready = probs(src_ref, grp, hh)
                if pending is not None:
                    accumulate(*pending)
                pending = ready
        accumulate(*pending)
        return carry

    n_groups = (i + GROUP - 1) // GROUP
    lax.fori_loop(0, (n_groups + 1) // 2, two_groups, 0)

    for hh in heads:
        acc = acc_ref[hh]
        o_ref[hh * HEAD_DIM:(hh + 1) * HEAD_DIM, :] = (
            acc[:HEAD_DIM] * (1.0 / acc[HEAD_DIM:HEAD_DIM + 1])).astype(o_ref.dtype)


def _sb_kernel(q_ref, k_ref, v_ref, o_ref, qz_ref, c_ref, acc_ref):
    i = pl.program_id(0)
    tq = q_ref.shape[-1]
    row = lax.broadcasted_iota(jnp.int32, (BLK, tq), 0)
    col = lax.broadcasted_iota(jnp.int32, (BLK, tq), 1)
    upper = jnp.where(row < col, 1.0, 0.0).astype(jnp.bfloat16)

    heads = range(N_HEADS)

    def block(j, diag):
        zs = [jnp.dot(_k_rows(k_ref, j, 1, hh), qz_ref[hh], preferred_element_type=jnp.float32)
              for hh in heads]
        log_sigs, log_fails, laters = [], [], []
        for hh in heads:
            z = zs[hh]
            log_sig = jnp.minimum(z, 0.0) - jnp.log(1.0 + jnp.exp(-jnp.abs(z)))
            log_fail = log_sig - z
            if diag:
                log_fail = jnp.where(row < col, log_fail, 0.0)
            laters.append(jnp.dot(upper, log_fail.astype(jnp.bfloat16),
                                  preferred_element_type=jnp.float32))
            log_sigs.append(log_sig)
            log_fails.append(log_fail)
        cmax = None
        for hh in heads:
            c = c_ref[hh]
            w = jnp.exp(log_sigs[hh] + laters[hh] + c)
            if diag:
                w = jnp.where(row < col, w, 0.0)
            acc_ref[hh] += jnp.dot(v_ref[hh, j], w.astype(jnp.bfloat16),
                                   preferred_element_type=jnp.float32)
            c_new = c + jnp.sum(log_fails[hh], axis=0, keepdims=True)
            c_ref[hh] = c_new
            cmax = c_new if cmax is None else jnp.maximum(cmax, c_new)
        return jnp.max(cmax)

    for hh in heads:
        qz_ref[hh] = _pad_pair(q_ref[hh, 0], hh)
    c_ref[...] = jnp.zeros_like(c_ref)
    acc_ref[...] = jnp.zeros_like(acc_ref)
    cmax = block(i, True)

    def cond(state):
        j, cmax = state
        return (j >= 0) & (cmax > SB_LOG_ZERO)

    def body(state):
        j, _ = state
        return j - 1, block(j, False)

    lax.while_loop(cond, body, (i - 1, cmax))
    for hh in range(N_HEADS):
        o_ref[hh * HEAD_DIM:(hh + 1) * HEAD_DIM, :] = acc_ref[hh].astype(o_ref.dtype)


def _merge_ffn_kernel(h_ref, ya_ref, yb_ref, gm_ref, wgate_ref, bg_ref, wa_ref, wb_ref, wo_ref,
                      g2_ref, wg_ref, wu_ref, wd_ref, gf_ref, o_ref):
    h = h_ref[...]
    d = h.shape[-1]
    u = _rms(h, gm_ref[...]).astype(jnp.bfloat16)
    gl = jnp.dot(u, wgate_ref[...], preferred_element_type=jnp.float32) + bg_ref[...]
    gates = 1.0 / (1.0 + jnp.exp(-gl))
    ma = lax.dot_general(ya_ref[...], wa_ref[...], _TN, preferred_element_type=jnp.float32)
    mb = lax.dot_general(yb_ref[...], wb_ref[...], _TN, preferred_element_type=jnp.float32)
    merged = gates[:, :d] * ma + gates[:, d:] * mb
    h = h + jnp.dot(merged.astype(jnp.bfloat16), wo_ref[...], preferred_element_type=jnp.float32)
    h = h + FFN_RES * _swiglu(_rms(h, g2_ref[...]), wg_ref, wu_ref, wd_ref)
    o_ref[...] = _rms(h, gf_ref[...])


def _resident(shape):
    return pl.BlockSpec(shape, lambda *_: (0,) * len(shape), pipeline_mode=pl.Buffered(1))


def kernel(x, positions, ffn1_norm, ffn1_w_gate, ffn1_w_up, ffn1_w_down, mix_norm, w_in, b_gate,
           w_branch_moba, w_branch_sb, w_out, ffn2_norm, ffn2_w_gate, ffn2_w_up, ffn2_w_down,
           final_norm):
    batch, seq, d = x.shape
    assert batch == 1 and ffn1_norm.shape[0] == 1 and seq % BLK == 0 and d == w_out.shape[-1]
    nb = seq // BLK
    d_ff = ffn1_w_gate.shape[-1]
    bf = jnp.bfloat16
    f32 = jnp.float32
    x2 = x[0]
    pos = positions.reshape(1, seq)
    w_in0 = w_in[0]
    a, b = D_ATT, 3 * D_ATT
    w_t = jnp.concatenate([w_in0[:, :b], w_in0[:, b:b + a], w_in0[:, b + 2 * a:b + 3 * a]], axis=1).T.astype(bf)
    w_kb = w_in0[:, b + a:b + 2 * a].astype(bf)
    w_gates = w_in0[:, 2 * b:].astype(bf)
    inv_freq = (ROPE_THETA ** (-jnp.arange(ROPE_HALF, dtype=f32) * 2.0 / ROPE_DIM)).reshape(ROPE_HALF, 1)
    row = lambda v: v.reshape(1, -1)

    cparams = functools.partial(pltpu.CompilerParams, vmem_limit_bytes=VMEM_LIMIT_BYTES)
    tok = lambda i: (i, 0)
    head_t = pl.BlockSpec((N_HEADS, 1, HEAD_DIM, BLK), lambda i: (0, i, 0, 0))
    t_shape = (N_HEADS, nb, HEAD_DIM, BLK)

    h1, qa, ka, kmean, va, qb, kb, vb = pl.pallas_call(
        _ffn_proj_kernel,
        grid=(nb,),
        in_specs=[
            pl.BlockSpec((BLK, d), tok),
            pl.BlockSpec((1, BLK), lambda i: (0, i)),
            _resident((ROPE_HALF, 1)),
            _resident((1, d)),
            _resident((d, d_ff)), _resident((d, d_ff)), _resident((d_ff, d)),
            _resident((1, d)),
            _resident((5 * D_ATT, d)), _resident((d, D_ATT)),
        ],
        out_specs=[
            pl.BlockSpec((BLK, d), tok),
            head_t,
            pl.BlockSpec((BLK, D_ATT), tok),
            pl.BlockSpec((1, 1, D_ATT), lambda i: (i, 0, 0)),
            head_t, head_t,
            pl.BlockSpec((BLK, D_ATT), tok),
            head_t,
        ],
        out_shape=[
            jax.ShapeDtypeStruct((seq, d), f32),
            jax.ShapeDtypeStruct(t_shape, f32),
            jax.ShapeDtypeStruct((seq, D_ATT), bf),
            jax.ShapeDtypeStruct((nb, 1, D_ATT), f32),
            jax.ShapeDtypeStruct(t_shape, bf),
            jax.ShapeDtypeStruct(t_shape, bf),
            jax.ShapeDtypeStruct((seq, D_ATT), bf),
            jax.ShapeDtypeStruct(t_shape, bf),
        ],
        compiler_params=cparams(dimension_semantics=("arbitrary",)),
        name="ffn1_proj",
    )(x2, pos, inv_freq, row(ffn1_norm[0]), ffn1_w_gate[0].astype(bf), ffn1_w_up[0].astype(bf),
      ffn1_w_down[0].astype(bf), row(mix_norm[0]), w_t, w_kb)

    kmean_h = kmean.reshape(nb, N_HEADS, HEAD_DIM).transpose(1, 0, 2)

    k_all = _resident((seq, D_ATT))
    v_all = _resident(t_shape)
    y_spec = pl.BlockSpec((D_ATT, BLK), lambda i: (0, i))
    y_shape = jax.ShapeDtypeStruct((D_ATT, seq), bf)
    qz_scratch = pltpu.VMEM((N_HEADS, 2 * HEAD_DIM, BLK), bf)

    ya = pl.pallas_call(
        _moba_kernel,
        grid=(nb,),
        in_specs=[head_t, k_all, v_all, _resident((N_HEADS, nb, HEAD_DIM))],
        out_specs=y_spec,
        out_shape=y_shape,
        scratch_shapes=[qz_scratch,
                        pltpu.VMEM((N_HEADS, nb, BLK), f32),
                        pltpu.VMEM((N_HEADS, 1, BLK), f32),
                        pltpu.VMEM((N_HEADS, HEAD_DIM + BF16_ROWS, BLK), f32),
                        pltpu.VMEM((N_HEADS, GROUP * BLK, BLK), f32),
                        pltpu.VMEM((N_HEADS, GROUP * BLK, BLK), f32)],
        compiler_params=cparams(dimension_semantics=("arbitrary",)),
        name="moba_attention",
    )(qa, ka, va, kmean_h)

    yb = pl.pallas_call(
        _sb_kernel,
        grid=(nb,),
        in_specs=[head_t, k_all, v_all],
        out_specs=y_spec,
        out_shape=y_shape,
        scratch_shapes=[qz_scratch,
                        pltpu.VMEM((N_HEADS, 1, BLK), f32),
                        pltpu.VMEM((N_HEADS, HEAD_DIM, BLK), f32)],
        compiler_params=cparams(dimension_semantics=("arbitrary",)),
        name="stick_breaking_attention",
    )(qb, kb, vb)

    out = pl.pallas_call(
        _merge_ffn_kernel,
        grid=(nb,),
        in_specs=[
            pl.BlockSpec((BLK, d), tok),
            pl.BlockSpec((D_ATT, BLK), lambda i: (0, i)),
            pl.BlockSpec((D_ATT, BLK), lambda i: (0, i)),
            _resident((1, d)),
            _resident((d, 2 * d)), _resident((1, 2 * d)),
            _resident((D_ATT, d)), _resident((D_ATT, d)), _resident((d, d)),
            _resident((1, d)),
            _resident((d, d_ff)), _resident((d, d_ff)), _resident((d_ff, d)),
            _resident((1, d)),
        ],
        out_specs=pl.BlockSpec((BLK, d), tok),
        out_shape=jax.ShapeDtypeStruct((seq, d), f32),
        compiler_params=cparams(dimension_semantics=("arbitrary",)),
        name="merge_ffn2",
    )(h1, ya, yb, row(mix_norm[0]), w_gates, row(b_gate[0]), w_branch_moba[0].astype(bf),
      w_branch_sb[0].astype(bf), w_out[0].astype(bf), row(ffn2_norm[0]), ffn2_w_gate[0].astype(bf),
      ffn2_w_up[0].astype(bf), ffn2_w_down[0].astype(bf), row(final_norm))
    return out[None]
```

```python
import functools

import jax
import jax.numpy as jnp
from jax import lax
from jax.experimental import pallas as pl
from jax.experimental.pallas import tpu as pltpu

HEAD_DIM = 64
N_HEADS = 8
D_ATT = N_HEADS * HEAD_DIM
LANES = 128
BF16_ROWS = 16
BLK = 256
TOP_K = 3
GROUP = 1
PHASES = 4
ROPE_THETA = 500000.0
ROPE_DIM = HEAD_DIM // 4
ROPE_HALF = ROPE_DIM // 2
FFN_RES = 0.5
EPS = 1e-6
NEG = -1e30
SCALE = HEAD_DIM ** -0.5
LOG2E = 1.4426950408889634
SB_LOG_ZERO = -104.0
VMEM_LIMIT_BYTES = 56 * 1024 * 1024

_NT = (((1,), (1,)), ((), ()))
_TN = (((0,), (0,)), ((), ()))


def _rms(x, g):
    return x * lax.rsqrt(jnp.mean(x * x, axis=-1, keepdims=True) + EPS) * g


def _swiglu(xn, wg_ref, wu_ref, wd_ref):
    xb = xn.astype(jnp.bfloat16)
    g = jnp.dot(xb, wg_ref[...], preferred_element_type=jnp.float32)
    u = jnp.dot(xb, wu_ref[...], preferred_element_type=jnp.float32)
    act = (g * (1.0 / (1.0 + jnp.exp(-g))) * u).astype(jnp.bfloat16)
    return jnp.dot(act, wd_ref[...], preferred_element_type=jnp.float32)


def _ffn_proj_kernel(x_ref, pos_ref, invf_ref, g1_ref, wg_ref, wu_ref, wd_ref, gm_ref,
                     wt_ref, wkb_ref,
                     h_ref, qa_ref, ka_ref, kmean_ref, va_ref, qb_ref, kb_ref, vb_ref):
    x = x_ref[...]
    h = x + FFN_RES * _swiglu(_rms(x, g1_ref[...]), wg_ref, wu_ref, wd_ref)
    h_ref[...] = h
    u = _rms(h, gm_ref[...]).astype(jnp.bfloat16)
    tm = u.shape[0]
    pt = lax.dot_general(wt_ref[...], u, _NT, preferred_element_type=jnp.float32)
    kb = jnp.dot(u, wkb_ref[...], preferred_element_type=jnp.float32)

    ang = invf_ref[...] * pos_ref[...].astype(jnp.float32)
    cos = jnp.cos(ang)[None]
    sin = jnp.sin(ang)[None]

    def rotary(t):
        x1 = t[:, :ROPE_HALF]
        x2 = t[:, ROPE_HALF:ROPE_DIM]
        return jnp.concatenate([x1 * cos - x2 * sin, x2 * cos + x1 * sin, t[:, ROPE_DIM:]], axis=1)

    def heads(lo):
        return pt[lo * D_ATT:(lo + 1) * D_ATT].reshape(N_HEADS, HEAD_DIM, tm)

    qa_ref[:, 0] = rotary(heads(0))
    ka = rotary(heads(1)).reshape(D_ATT, tm).T
    kmean_ref[0] = jnp.mean(ka, axis=0, keepdims=True)
    ka_ref[...] = ka.astype(jnp.bfloat16)
    kb_ref[...] = kb.astype(jnp.bfloat16)
    va_ref[:, 0] = heads(2).astype(jnp.bfloat16)
    qb_ref[:, 0] = (heads(3) * SCALE).astype(jnp.bfloat16)
    vb_ref[:, 0] = heads(4).astype(jnp.bfloat16)


def _pad_pair(q, hh):
    z = jnp.zeros_like(q)
    return jnp.concatenate([q, z] if hh % 2 == 0 else [z, q], axis=0)


def _k_rows(k_ref, j, nblk, hh):
    p = hh // 2
    return k_ref[pl.ds(pl.multiple_of(j * BLK, BLK), nblk * BLK), p * LANES:(p + 1) * LANES]


def _moba_kernel(q_ref, k_ref, v_ref, kmean_ref, o_ref, qz_ref, bias_ref, m_ref, acc_ref,
                 sa_ref, sb_ref):
    i = pl.program_id(0)
    nb, tq = bias_ref.shape[1:]
    blk = lax.broadcasted_iota(jnp.int32, (nb, tq), 0)
    heads = range(N_HEADS)

    def v_aug(hh, j, nblk):
        ones = jnp.ones((BF16_ROWS, nblk * BLK), jnp.bfloat16)
        return jnp.concatenate(
            [jnp.concatenate([v_ref[hh, j + u] for u in range(nblk)], axis=1), ones], axis=0)

    q32 = [q_ref[hh, 0] for hh in heads]
    gates = [jnp.dot(kmean_ref[hh], q32[hh], preferred_element_type=jnp.float32,
                     precision=lax.Precision.HIGHEST) for hh in heads]
    for hh in heads:
        qz_ref[hh] = _pad_pair((q32[hh] * (SCALE * LOG2E)).astype(jnp.bfloat16), hh)
    for hh in heads:
        g = jnp.where(blk < i, gates[hh], NEG)
        sel = jnp.zeros((nb, tq), jnp.bool_)
        for _ in range(TOP_K):
            mx = jnp.max(g, axis=0, keepdims=True)
            first = jnp.min(jnp.where(g == mx, blk, nb), axis=0, keepdims=True)
            pick = blk == first
            sel = sel | pick
            g = jnp.where(pick, -jnp.inf, g)
        bias_ref[hh] = jnp.where(sel & (blk < i), 0.0, NEG)

    row = lax.broadcasted_iota(jnp.int32, (BLK, tq), 0)
    col = lax.broadcasted_iota(jnp.int32, (BLK, tq), 1)
    s_own = [jnp.dot(_k_rows(k_ref, i, 1, hh), qz_ref[hh], preferred_element_type=jnp.float32)
             for hh in heads]
    for hh in heads:
        s = jnp.where(row <= col, s_own[hh], NEG)
        m = jnp.max(s, axis=0, keepdims=True)
        p = jnp.exp2(s - m)
        m_ref[hh] = m
        acc_ref[hh] = jnp.dot(v_aug(hh, i, 1), p.astype(jnp.bfloat16),
                              preferred_element_type=jnp.float32)

    n_groups_max = nb // GROUP

    def scores_into(dst_ref, grp, hh):
        dst_ref[hh] = jnp.dot(_k_rows(k_ref, grp * GROUP, GROUP, hh), qz_ref[hh],
                              preferred_element_type=jnp.float32)

    def probs(src_ref, grp, hh):
        j = grp * GROUP
        parts = [src_ref[hh, u * BLK:(u + 1) * BLK] for u in range(GROUP)]
        biases = [bias_ref[hh, pl.ds(j + u, 1), :] for u in range(GROUP)]
        m_old = m_ref[hh]
        m_new = m_old
        for s_u, b_u in zip(parts, biases):
            m_new = jnp.maximum(m_new, jnp.max(s_u, axis=0, keepdims=True) + b_u)
        p = jnp.concatenate([jnp.exp2((s_u - (m_new - b_u)).astype(jnp.bfloat16))
                             for s_u, b_u in zip(parts, biases)], axis=0)
        m_ref[hh] = m_new
        return hh, j, jnp.exp2(m_old - m_new), p

    def accumulate(hh, j, alpha, p):
        acc_ref[hh] = alpha * acc_ref[hh] + jnp.dot(v_aug(hh, j, GROUP), p,
                                                   preferred_element_type=jnp.float32)

    for hh in heads:
        scores_into(sa_ref, 0, hh)

    def several_groups(t, carry):
        g0 = PHASES * t
        pending = None
        for ph in range(PHASES):
            src_ref, dst_ref = (sa_ref, sb_ref) if ph % 2 == 0 else (sb_ref, sa_ref)
            grp_next = jnp.minimum(g0 + ph + 1, n_groups_max - 1)
            for hh in heads:
                scores_into(dst_ref, grp_next, hh)
                ready = probs(src_ref, g0 + ph, hh)
                if pending is not None:
                    accumulate(*pending)
                pending = ready
        accumulate(*pending)
        return carry

    n_groups = (i + GROUP - 1) // GROUP
    lax.fori_loop(0, (n_groups + PHASES - 1) // PHASES, several_groups, 0)

    for hh in heads:
        acc = acc_ref[hh]
        o_ref[hh * HEAD_DIM:(hh + 1) * HEAD_DIM, :] = (
            acc[:HEAD_DIM] * (1.0 / acc[HEAD_DIM:HEAD_DIM + 1])).astype(o_ref.dtype)


def _sb_kernel(q_ref, k_ref, v_ref, o_ref, qz_ref, c_ref, acc_ref):
    i = pl.program_id(0)
    tq = q_ref.shape[-1]
    row = lax.broadcasted_iota(jnp.int32, (BLK, tq), 0)
    col = lax.broadcasted_iota(jnp.int32, (BLK, tq), 1)
    upper = jnp.where(row < col, 1.0, 0.0).astype(jnp.bfloat16)

    heads = range(N_HEADS)

    def block(j, diag):
        zs = [jnp.dot(_k_rows(k_ref, j, 1, hh), qz_ref[hh], preferred_element_type=jnp.float32)
              for hh in heads]
        log_sigs, log_fails, laters = [], [], []
        for hh in heads:
            z = zs[hh]
            log_sig = jnp.minimum(z, 0.0) - jnp.log(1.0 + jnp.exp(-jnp.abs(z)))
            log_fail = log_sig - z
            if diag:
                log_fail = jnp.where(row < col, log_fail, 0.0)
            laters.append(jnp.dot(upper, log_fail.astype(jnp.bfloat16),
                                  preferred_element_type=jnp.float32))
            log_sigs.append(log_sig)
            log_fails.append(log_fail)
        cmax = None
        for hh in heads:
            c = c_ref[hh]
            w = jnp.exp(log_sigs[hh] + laters[hh] + c)
            if diag:
                w = jnp.where(row < col, w, 0.0)
            acc_ref[hh] += jnp.dot(v_ref[hh, j], w.astype(jnp.bfloat16),
                                   preferred_element_type=jnp.float32)
            c_new = c + jnp.sum(log_fails[hh], axis=0, keepdims=True)
            c_ref[hh] = c_new
            cmax = c_new if cmax is None else jnp.maximum(cmax, c_new)
        return jnp.max(cmax)

    for hh in heads:
        qz_ref[hh] = _pad_pair(q_ref[hh, 0], hh)
    c_ref[...] = jnp.zeros_like(c_ref)
    acc_ref[...] = jnp.zeros_like(acc_ref)
    cmax = block(i, True)

    def cond(state):
        j, cmax = state
        return (j >= 0) & (cmax > SB_LOG_ZERO)

    def body(state):
        j, _ = state
        return j - 1, block(j, False)

    lax.while_loop(cond, body, (i - 1, cmax))
    for hh in range(N_HEADS):
        o_ref[hh * HEAD_DIM:(hh + 1) * HEAD_DIM, :] = acc_ref[hh].astype(o_ref.dtype)


def _merge_ffn_kernel(h_ref, ya_ref, yb_ref, gm_ref, wgate_ref, bg_ref, wa_ref, wb_ref, wo_ref,
                      g2_ref, wg_ref, wu_ref, wd_ref, gf_ref, o_ref):
    h = h_ref[...]
    d = h.shape[-1]
    u = _rms(h, gm_ref[...]).astype(jnp.bfloat16)
    gl = jnp.dot(u, wgate_ref[...], preferred_element_type=jnp.float32) + bg_ref[...]
    gates = 1.0 / (1.0 + jnp.exp(-gl))
    ma = lax.dot_general(ya_ref[...], wa_ref[...], _TN, preferred_element_type=jnp.float32)
    mb = lax.dot_general(yb_ref[...], wb_ref[...], _TN, preferred_element_type=jnp.float32)
    merged = gates[:, :d] * ma + gates[:, d:] * mb
    h = h + jnp.dot(merged.astype(jnp.bfloat16), wo_ref[...], preferred_element_type=jnp.float32)
    h = h + FFN_RES * _swiglu(_rms(h, g2_ref[...]), wg_ref, wu_ref, wd_ref)
    o_ref[...] = _rms(h, gf_ref[...])


def _resident(shape):
    return pl.BlockSpec(shape, lambda *_: (0,) * len(shape), pipeline_mode=pl.Buffered(1))


def kernel(x, positions, ffn1_norm, ffn1_w_gate, ffn1_w_up, ffn1_w_down, mix_norm, w_in, b_gate,
           w_branch_moba, w_branch_sb, w_out, ffn2_norm, ffn2_w_gate, ffn2_w_up, ffn2_w_down,
           final_norm):
    batch, seq, d = x.shape
    assert batch == 1 and ffn1_norm.shape[0] == 1 and seq % BLK == 0 and d == w_out.shape[-1]
    nb = seq // BLK
    d_ff = ffn1_w_gate.shape[-1]
    bf = jnp.bfloat16
    f32 = jnp.float32
    x2 = x[0]
    pos = positions.reshape(1, seq)
    w_in0 = w_in[0]
    a, b = D_ATT, 3 * D_ATT
    w_t = jnp.concatenate([w_in0[:, :b], w_in0[:, b:b + a], w_in0[:, b + 2 * a:b + 3 * a]], axis=1).T.astype(bf)
    w_kb = w_in0[:, b + a:b + 2 * a].astype(bf)
    w_gates = w_in0[:, 2 * b:].astype(bf)
    inv_freq = (ROPE_THETA ** (-jnp.arange(ROPE_HALF, dtype=f32) * 2.0 / ROPE_DIM)).reshape(ROPE_HALF, 1)
    row = lambda v: v.reshape(1, -1)

    cparams = functools.partial(pltpu.CompilerParams, vmem_limit_bytes=VMEM_LIMIT_BYTES)
    tok = lambda i: (i, 0)
    head_t = pl.BlockSpec((N_HEADS, 1, HEAD_DIM, BLK), lambda i: (0, i, 0, 0))
    t_shape = (N_HEADS, nb, HEAD_DIM, BLK)

    h1, qa, ka, kmean, va, qb, kb, vb = pl.pallas_call(
        _ffn_proj_kernel,
        grid=(nb,),
        in_specs=[
            pl.BlockSpec((BLK, d), tok),
            pl.BlockSpec((1, BLK), lambda i: (0, i)),
            _resident((ROPE_HALF, 1)),
            _resident((1, d)),
            _resident((d, d_ff)), _resident((d, d_ff)), _resident((d_ff, d)),
            _resident((1, d)),
            _resident((5 * D_ATT, d)), _resident((d, D_ATT)),
        ],
        out_specs=[
            pl.BlockSpec((BLK, d), tok),
            head_t,
            pl.BlockSpec((BLK, D_ATT), tok),
            pl.BlockSpec((1, 1, D_ATT), lambda i: (i, 0, 0)),
            head_t, head_t,
            pl.BlockSpec((BLK, D_ATT), tok),
            head_t,
        ],
        out_shape=[
            jax.ShapeDtypeStruct((seq, d), f32),
            jax.ShapeDtypeStruct(t_shape, f32),
            jax.ShapeDtypeStruct((seq, D_ATT), bf),
            jax.ShapeDtypeStruct((nb, 1, D_ATT), f32),
            jax.ShapeDtypeStruct(t_shape, bf),
            jax.ShapeDtypeStruct(t_shape, bf),
            jax.ShapeDtypeStruct((seq, D_ATT), bf),
            jax.ShapeDtypeStruct(t_shape, bf),
        ],
        compiler_params=cparams(dimension_semantics=("arbitrary",)),
        name="ffn1_proj",
    )(x2, pos, inv_freq, row(ffn1_norm[0]), ffn1_w_gate[0].astype(bf), ffn1_w_up[0].astype(bf),
      ffn1_w_down[0].astype(bf), row(mix_norm[0]), w_t, w_kb)

    kmean_h = kmean.reshape(nb, N_HEADS, HEAD_DIM).transpose(1, 0, 2)

    k_all = _resident((seq, D_ATT))
    v_all = _resident(t_shape)
    y_spec = pl.BlockSpec((D_ATT, BLK), lambda i: (0, i))
    y_shape = jax.ShapeDtypeStruct((D_ATT, seq), bf)
    qz_scratch = pltpu.VMEM((N_HEADS, 2 * HEAD_DIM, BLK), bf)

    ya = pl.pallas_call(
        _moba_kernel,
        grid=(nb,),
        in_specs=[head_t, k_all, v_all, _resident((N_HEADS, nb, HEAD_DIM))],
        out_specs=y_spec,
        out_shape=y_shape,
        scratch_shapes=[qz_scratch,
                        pltpu.VMEM((N_HEADS, nb, BLK), f32),
                        pltpu.VMEM((N_HEADS, 1, BLK), f32),
                        pltpu.VMEM((N_HEADS, HEAD_DIM + BF16_ROWS, BLK), f32),
                        pltpu.VMEM((N_HEADS, GROUP * BLK, BLK), f32),
                        pltpu.VMEM((N_HEADS, GROUP * BLK, BLK), f32)],
        compiler_params=cparams(dimension_semantics=("arbitrary",)),
        name="moba_attention",
    )(qa, ka, va, kmean_h)

    yb = pl.pallas_call(
        _sb_kernel,
        grid=(nb,),
        in_specs=[head_t, k_all, v_all],
        out_specs=y_spec,
        out_shape=y_shape,
        scratch_shapes=[qz_scratch,
                        pltpu.VMEM((N_HEADS, 1, BLK), f32),
                        pltpu.VMEM((N_HEADS, HEAD_DIM, BLK), f32)],
        compiler_params=cparams(dimension_semantics=("arbitrary",)),
        name="stick_breaking_attention",
    )(qb, kb, vb)

    out = pl.pallas_call(
        _merge_ffn_kernel,
        grid=(nb,),
        in_specs=[
            pl.BlockSpec((BLK, d), tok),
            pl.BlockSpec((D_ATT, BLK), lambda i: (0, i)),
            pl.BlockSpec((D_ATT, BLK), lambda i: (0, i)),
            _resident((1, d)),
            _resident((d, 2 * d)), _resident((1, 2 * d)),
            _resident((D_ATT, d)), _resident((D_ATT, d)), _resident((d, d)),
            _resident((1, d)),
            _resident((d, d_ff)), _resident((d, d_ff)), _resident((d_ff, d)),
            _resident((1, d)),
        ],
        out_specs=pl.BlockSpec((BLK, d), tok),
        out_shape=jax.ShapeDtypeStruct((seq, d), f32),
        compiler_params=cparams(dimension_semantics=("arbitrary",)),
        name="merge_ffn2",
    )(h1, ya, yb, row(mix_norm[0]), w_gates, row(b_gate[0]), w_branch_moba[0].astype(bf),
      w_branch_sb[0].astype(bf), w_out[0].astype(bf), row(ffn2_norm[0]), ffn2_w_gate[0].astype(bf),
      ffn2_w_up[0].astype(bf), ffn2_w_down[0].astype(bf), row(final_norm))
    return out[None]
```

```python
import functools

import jax
import jax.numpy as jnp
from jax import lax
from jax.experimental import pallas as pl
from jax.experimental.pallas import tpu as pltpu

HEAD_DIM = 64
N_HEADS = 8
D_ATT = N_HEADS * HEAD_DIM
LANES = 128
BF16_ROWS = 16
BLK = 256
TOP_K = 3
GROUP = 1
PHASES = 4
ROPE_THETA = 500000.0
ROPE_DIM = HEAD_DIM // 4
ROPE_HALF = ROPE_DIM // 2
FFN_RES = 0.5
EPS = 1e-6
NEG = -1e30
SCALE = HEAD_DIM ** -0.5
LOG2E = 1.4426950408889634
SB_LOG_ZERO = -104.0
VMEM_LIMIT_BYTES = 56 * 1024 * 1024

_NT = (((1,), (1,)), ((), ()))
_TN = (((0,), (0,)), ((), ()))


def _rms(x, g):
    return x * lax.rsqrt(jnp.mean(x * x, axis=-1, keepdims=True) + EPS) * g


def _swiglu(xn, wg_ref, wu_ref, wd_ref):
    xb = xn.astype(jnp.bfloat16)
    g = jnp.dot(xb, wg_ref[...], preferred_element_type=jnp.float32)
    u = jnp.dot(xb, wu_ref[...], preferred_element_type=jnp.float32)
    act = (g * (1.0 / (1.0 + jnp.exp(-g))) * u).astype(jnp.bfloat16)
    return jnp.dot(act, wd_ref[...], preferred_element_type=jnp.float32)


def _ffn_proj_kernel(x_ref, pos_ref, invf_ref, g1_ref, wg_ref, wu_ref, wd_ref, gm_ref,
                     wt_ref, wkb_ref,
                     h_ref, qa_ref, ka_ref, kmean_ref, va_ref, qb_ref, kb_ref, vb_ref):
    x = x_ref[...]
    h = x + FFN_RES * _swiglu(_rms(x, g1_ref[...]), wg_ref, wu_ref, wd_ref)
    h_ref[...] = h
    u = _rms(h, gm_ref[...]).astype(jnp.bfloat16)
    tm = u.shape[0]
    pt = lax.dot_general(wt_ref[...], u, _NT, preferred_element_type=jnp.float32)
    kb = jnp.dot(u, wkb_ref[...], preferred_element_type=jnp.float32)

    ang = invf_ref[...] * pos_ref[...].astype(jnp.float32)
    cos = jnp.cos(ang)[None]
    sin = jnp.sin(ang)[None]

    def rotary(t):
        x1 = t[:, :ROPE_HALF]
        x2 = t[:, ROPE_HALF:ROPE_DIM]
        return jnp.concatenate([x1 * cos - x2 * sin, x2 * cos + x1 * sin, t[:, ROPE_DIM:]], axis=1)

    def heads(lo):
        return pt[lo * D_ATT:(lo + 1) * D_ATT].reshape(N_HEADS, HEAD_DIM, tm)

    qa_ref[:, 0] = rotary(heads(0))
    ka = rotary(heads(1)).reshape(D_ATT, tm).T
    kmean_ref[0] = jnp.mean(ka, axis=0, keepdims=True)
    ka_ref[...] = ka.astype(jnp.bfloat16)
    kb_ref[...] = kb.astype(jnp.bfloat16)
    va_ref[:, 0] = heads(2).astype(jnp.bfloat16)
    qb_ref[:, 0] = (heads(3) * SCALE).astype(jnp.bfloat16)
    vb_ref[:, 0] = heads(4).astype(jnp.bfloat16)


def _pad_pair(q, hh):
    z = jnp.zeros_like(q)
    return jnp.concatenate([q, z] if hh % 2 == 0 else [z, q], axis=0)


def _k_rows(k_ref, j, nblk, hh):
    p = hh // 2
    return k_ref[pl.ds(pl.multiple_of(j * BLK, BLK), nblk * BLK), p * LANES:(p + 1) * LANES]


def _moba_kernel(q_ref, k_ref, v_ref, kmean_ref, o_ref, qz_ref, bias_ref, m_ref, acc_ref,
                 sa_ref, sb_ref):
    i = pl.program_id(0)
    nb, tq = bias_ref.shape[1:]
    blk = lax.broadcasted_iota(jnp.int32, (nb, tq), 0)
    heads = range(N_HEADS)

    def v_aug(hh, j, nblk):
        ones = jnp.ones((BF16_ROWS, nblk * BLK), jnp.bfloat16)
        return jnp.concatenate(
            [jnp.concatenate([v_ref[hh, j + u] for u in range(nblk)], axis=1), ones], axis=0)

    q32 = [q_ref[hh, 0] for hh in heads]
    gates = [jnp.dot(kmean_ref[hh], q32[hh], preferred_element_type=jnp.float32,
                     precision=lax.Precision.HIGHEST) for hh in heads]
    for hh in heads:
        qz_ref[hh] = _pad_pair((q32[hh] * (SCALE * LOG2E)).astype(jnp.bfloat16), hh)
    for hh in heads:
        g = jnp.where(blk < i, gates[hh], NEG)
        sel = jnp.zeros((nb, tq), jnp.bool_)
        for _ in range(TOP_K):
            mx = jnp.max(g, axis=0, keepdims=True)
            first = jnp.min(jnp.where(g == mx, blk, nb), axis=0, keepdims=True)
            pick = blk == first
            sel = sel | pick
            g = jnp.where(pick, -jnp.inf, g)
        bias_ref[hh] = jnp.where(sel & (blk < i), 0.0, NEG)

    row = lax.broadcasted_iota(jnp.int32, (BLK, tq), 0)
    col = lax.broadcasted_iota(jnp.int32, (BLK, tq), 1)
    s_own = [jnp.dot(_k_rows(k_ref, i, 1, hh), qz_ref[hh], preferred_element_type=jnp.float32)
             for hh in heads]
    for hh in heads:
        s = jnp.where(row <= col, s_own[hh], NEG)
        m = jnp.max(s, axis=0, keepdims=True)
        p = jnp.exp2(s - m)
        m_ref[hh] = m
        acc_ref[hh] = jnp.dot(v_aug(hh, i, 1), p.astype(jnp.bfloat16),
                              preferred_element_type=jnp.float32)

    n_groups_max = nb // GROUP

    def scores_into(dst_ref, grp, hh):
        dst_ref[hh] = jnp.dot(_k_rows(k_ref, grp * GROUP, GROUP, hh), qz_ref[hh],
                              preferred_element_type=jnp.float32)

    def probs(src_ref, grp, hh):
        j = grp * GROUP
        parts = [src_ref[hh, u * BLK:(u + 1) * BLK] for u in range(GROUP)]
        biases = [bias_ref[hh, pl.ds(j + u, 1), :] for u in range(GROUP)]
        m_old = m_ref[hh]
        m_new = m_old
        for s_u, b_u in zip(parts, biases):
            m_new = jnp.maximum(m_new, jnp.max(s_u, axis=0, keepdims=True) + b_u)
        p = jnp.concatenate([jnp.exp2((s_u - (m_new - b_u)).astype(jnp.bfloat16))
                             for s_u, b_u in zip(parts, biases)], axis=0)
        m_ref[hh] = m_new
        return hh, j, jnp.exp2(m_old - m_new), p

    def accumulate(hh, j, alpha, p):
        acc_ref[hh] = alpha * acc_ref[hh] + jnp.dot(v_aug(hh, j, GROUP), p,
                                                   preferred_element_type=jnp.float32)

    for hh in heads:
        scores_into(sa_ref, 0, hh)

    def walk(g0, phases, prefetch_last):
        pending = None
        for ph in range(phases):
            src_ref, dst_ref = (sa_ref, sb_ref) if ph % 2 == 0 else (sb_ref, sa_ref)
            grp_next = jnp.minimum(g0 + ph + 1, n_groups_max - 1)
            for hh in heads:
                if prefetch_last or ph + 1 < phases:
                    scores_into(dst_ref, grp_next, hh)
                ready = probs(src_ref, g0 + ph, hh)
                if pending is not None:
                    accumulate(*pending)
                pending = ready
        accumulate(*pending)

    def several_groups(t, carry):
        walk(PHASES * t, PHASES, True)
        return carry

    n_groups = (i + GROUP - 1) // GROUP
    rest = n_groups % PHASES
    n_full = n_groups // PHASES + jnp.where(rest > PHASES // 2, 1, 0)
    lax.fori_loop(0, n_full, several_groups, 0)

    @pl.when((rest > 0) & (rest <= PHASES // 2))
    def _():
        walk(PHASES * n_full, PHASES // 2, False)

    for hh in heads:
        acc = acc_ref[hh]
        o_ref[hh * HEAD_DIM:(hh + 1) * HEAD_DIM, :] = (
            acc[:HEAD_DIM] * (1.0 / acc[HEAD_DIM:HEAD_DIM + 1])).astype(o_ref.dtype)


def _sb_kernel(q_ref, k_ref, v_ref, o_ref, qz_ref, c_ref, acc_ref):
    i = pl.program_id(0)
    tq = q_ref.shape[-1]
    row = lax.broadcasted_iota(jnp.int32, (BLK, tq), 0)
    col = lax.broadcasted_iota(jnp.int32, (BLK, tq), 1)
    upper = jnp.where(row < col, 1.0, 0.0).astype(jnp.bfloat16)

    heads = range(N_HEADS)

    def block(j, diag):
        zs = [jnp.dot(_k_rows(k_ref, j, 1, hh), qz_ref[hh], preferred_element_type=jnp.float32)
              for hh in heads]
        log_sigs, log_fails, laters = [], [], []
        for hh in heads:
            z = zs[hh]
            log_sig = jnp.minimum(z, 0.0) - jnp.log(1.0 + jnp.exp(-jnp.abs(z)))
            log_fail = log_sig - z
            if diag:
                log_fail = jnp.where(row < col, log_fail, 0.0)
            laters.append(jnp.dot(upper, log_fail.astype(jnp.bfloat16),
                                  preferred_element_type=jnp.float32))
            log_sigs.append(log_sig)
            log_fails.append(log_fail)
        cmax = None
        for hh in heads:
            c = c_ref[hh]
            w = jnp.exp(log_sigs[hh] + laters[hh] + c)
            if diag:
                w = jnp.where(row < col, w, 0.0)
            acc_ref[hh] += jnp.dot(v_ref[hh, j], w.astype(jnp.bfloat16),
                                   preferred_element_type=jnp.float32)
            c_new = c + jnp.sum(log_fails[hh], axis=0, keepdims=True)
            c_ref[hh] = c_new
            cmax = c_new if cmax is None else jnp.maximum(cmax, c_new)
        return jnp.max(cmax)

    for hh in heads:
        qz_ref[hh] = _pad_pair(q_ref[hh, 0], hh)
    c_ref[...] = jnp.zeros_like(c_ref)
    acc_ref[...] = jnp.zeros_like(acc_ref)
    cmax = block(i, True)

    def cond(state):
        j, cmax = state
        return (j >= 0) & (cmax > SB_LOG_ZERO)

    def body(state):
        j, _ = state
        return j - 1, block(j, False)

    lax.while_loop(cond, body, (i - 1, cmax))
    for hh in range(N_HEADS):
        o_ref[hh * HEAD_DIM:(hh + 1) * HEAD_DIM, :] = acc_ref[hh].astype(o_ref.dtype)


def _merge_ffn_kernel(h_ref, ya_ref, yb_ref, gm_ref, wgate_ref, bg_ref, wa_ref, wb_ref, wo_ref,
                      g2_ref, wg_ref, wu_ref, wd_ref, gf_ref, o_ref):
    h = h_ref[...]
    d = h.shape[-1]
    u = _rms(h, gm_ref[...]).astype(jnp.bfloat16)
    gl = jnp.dot(u, wgate_ref[...], preferred_element_type=jnp.float32) + bg_ref[...]
    gates = 1.0 / (1.0 + jnp.exp(-gl))
    ma = lax.dot_general(ya_ref[...], wa_ref[...], _TN, preferred_element_type=jnp.float32)
    mb = lax.dot_general(yb_ref[...], wb_ref[...], _TN, preferred_element_type=jnp.float32)
    merged = gates[:, :d] * ma + gates[:, d:] * mb
    h = h + jnp.dot(merged.astype(jnp.bfloat16), wo_ref[...], preferred_element_type=jnp.float32)
    h = h + FFN_RES * _swiglu(_rms(h, g2_ref[...]), wg_ref, wu_ref, wd_ref)
    o_ref[...] = _rms(h, gf_ref[...])


def _resident(shape):
    return pl.BlockSpec(shape, lambda *_: (0,) * len(shape), pipeline_mode=pl.Buffered(1))


def kernel(x, positions, ffn1_norm, ffn1_w_gate, ffn1_w_up, ffn1_w_down, mix_norm, w_in, b_gate,
           w_branch_moba, w_branch_sb, w_out, ffn2_norm, ffn2_w_gate, ffn2_w_up, ffn2_w_down,
           final_norm):
    batch, seq, d = x.shape
    assert batch == 1 and ffn1_norm.shape[0] == 1 and seq % BLK == 0 and d == w_out.shape[-1]
    nb = seq // BLK
    d_ff = ffn1_w_gate.shape[-1]
    bf = jnp.bfloat16
    f32 = jnp.float32
    x2 = x[0]
    pos = positions.reshape(1, seq)
    w_in0 = w_in[0]
    a, b = D_ATT, 3 * D_ATT
    w_t = jnp.concatenate([w_in0[:, :b], w_in0[:, b:b + a], w_in0[:, b + 2 * a:b + 3 * a]], axis=1).T.astype(bf)
    w_kb = w_in0[:, b + a:b + 2 * a].astype(bf)
    w_gates = w_in0[:, 2 * b:].astype(bf)
    inv_freq = (ROPE_THETA ** (-jnp.arange(ROPE_HALF, dtype=f32) * 2.0 / ROPE_DIM)).reshape(ROPE_HALF, 1)
    row = lambda v: v.reshape(1, -1)

    cparams = functools.partial(pltpu.CompilerParams, vmem_limit_bytes=VMEM_LIMIT_BYTES)
    tok = lambda i: (i, 0)
    head_t = pl.BlockSpec((N_HEADS, 1, HEAD_DIM, BLK), lambda i: (0, i, 0, 0))
    t_shape = (N_HEADS, nb, HEAD_DIM, BLK)

    h1, qa, ka, kmean, va, qb, kb, vb = pl.pallas_call(
        _ffn_proj_kernel,
        grid=(nb,),
        in_specs=[
            pl.BlockSpec((BLK, d), tok),
            pl.BlockSpec((1, BLK), lambda i: (0, i)),
            _resident((ROPE_HALF, 1)),
            _resident((1, d)),
            _resident((d, d_ff)), _resident((d, d_ff)), _resident((d_ff, d)),
            _resident((1, d)),
            _resident((5 * D_ATT, d)), _resident((d, D_ATT)),
        ],
        out_specs=[
            pl.BlockSpec((BLK, d), tok),
            head_t,
            pl.BlockSpec((BLK, D_ATT), tok),
            pl.BlockSpec((1, 1, D_ATT), lambda i: (i, 0, 0)),
            head_t, head_t,
            pl.BlockSpec((BLK, D_ATT), tok),
            head_t,
        ],
        out_shape=[
            jax.ShapeDtypeStruct((seq, d), f32),
            jax.ShapeDtypeStruct(t_shape, f32),
            jax.ShapeDtypeStruct((seq, D_ATT), bf),
            jax.ShapeDtypeStruct((nb, 1, D_ATT), f32),
            jax.ShapeDtypeStruct(t_shape, bf),
            jax.ShapeDtypeStruct(t_shape, bf),
            jax.ShapeDtypeStruct((seq, D_ATT), bf),
            jax.ShapeDtypeStruct(t_shape, bf),
        ],
        compiler_params=cparams(dimension_semantics=("arbitrary",)),
        name="ffn1_proj",
    )(x2, pos, inv_freq, row(ffn1_norm[0]), ffn1_w_gate[0].astype(bf), ffn1_w_up[0].astype(bf),
      ffn1_w_down[0].astype(bf), row(mix_norm[0]), w_t, w_kb)

    kmean_h = kmean.reshape(nb, N_HEADS, HEAD_DIM).transpose(1, 0, 2)

    k_all = _resident((seq, D_ATT))
    v_all = _resident(t_shape)
    y_spec = pl.BlockSpec((D_ATT, BLK), lambda i: (0, i))
    y_shape = jax.ShapeDtypeStruct((D_ATT, seq), bf)
    qz_scratch = pltpu.VMEM((N_HEADS, 2 * HEAD_DIM, BLK), bf)

    ya = pl.pallas_call(
        _moba_kernel,
        grid=(nb,),
        in_specs=[head_t, k_all, v_all, _resident((N_HEADS, nb, HEAD_DIM))],
        out_specs=y_spec,
        out_shape=y_shape,
        scratch_shapes=[qz_scratch,
                        pltpu.VMEM((N_HEADS, nb, BLK), f32),
                        pltpu.VMEM((N_HEADS, 1, BLK), f32),
                        pltpu.VMEM((N_HEADS, HEAD_DIM + BF16_ROWS, BLK), f32),
                        pltpu.VMEM((N_HEADS, GROUP * BLK, BLK), f32),
                        pltpu.VMEM((N_HEADS, GROUP * BLK, BLK), f32)],
        compiler_params=cparams(dimension_semantics=("arbitrary",)),
        name="moba_attention",
    )(qa, ka, va, kmean_h)

    yb = pl.pallas_call(
        _sb_kernel,
        grid=(nb,),
        in_specs=[head_t, k_all, v_all],
        out_specs=y_spec,
        out_shape=y_shape,
        scratch_shapes=[qz_scratch,
                        pltpu.VMEM((N_HEADS, 1, BLK), f32),
                        pltpu.VMEM((N_HEADS, HEAD_DIM, BLK), f32)],
        compiler_params=cparams(dimension_semantics=("arbitrary",)),
        name="stick_breaking_attention",
    )(qb, kb, vb)

    out = pl.pallas_call(
        _merge_ffn_kernel,
        grid=(nb,),
        in_specs=[
            pl.BlockSpec((BLK, d), tok),
            pl.BlockSpec((D_ATT, BLK), lambda i: (0, i)),
            pl.BlockSpec((D_ATT, BLK), lambda i: (0, i)),
            _resident((1, d)),
            _resident((d, 2 * d)), _resident((1, 2 * d)),
            _resident((D_ATT, d)), _resident((D_ATT, d)), _resident((d, d)),
            _resident((1, d)),
            _resident((d, d_ff)), _resident((d, d_ff)), _resident((d_ff, d)),
            _resident((1, d)),
        ],
        out_specs=pl.BlockSpec((BLK, d), tok),
        out_shape=jax.ShapeDtypeStruct((seq, d), f32),
        compiler_params=cparams(dimension_semantics=("arbitrary",)),
        name="merge_ffn2",
    )(h1, ya, yb, row(mix_norm[0]), w_gates, row(b_gate[0]), w_branch_moba[0].astype(bf),
      w_branch_sb[0].astype(bf), w_out[0].astype(bf), row(ffn2_norm[0]), ffn2_w_gate[0].astype(bf),
      ffn2_w_up[0].astype(bf), ffn2_w_down[0].astype(bf), row(final_norm))
    return out[None]
```

```python
import functools

import jax
import jax.numpy as jnp
from jax import lax
from jax.experimental import pallas as pl
from jax.experimental.pallas import tpu as pltpu

HEAD_DIM = 64
N_HEADS = 8
D_ATT = N_HEADS * HEAD_DIM
LANES = 128
BF16_ROWS = 16
BLK = 256
TOP_K = 3
GROUP = 1
PHASES = 4
ROPE_THETA = 500000.0
ROPE_DIM = HEAD_DIM // 4
ROPE_HALF = ROPE_DIM // 2
FFN_RES = 0.5
EPS = 1e-6
NEG = -1e30
SCALE = HEAD_DIM ** -0.5
LOG2E = 1.4426950408889634
SB_LOG2_ZERO = -151.0
VMEM_LIMIT_BYTES = 56 * 1024 * 1024

_NT = (((1,), (1,)), ((), ()))
_TN = (((0,), (0,)), ((), ()))


def _rms(x, g):
    return x * lax.rsqrt(jnp.mean(x * x, axis=-1, keepdims=True) + EPS) * g


def _swiglu(xn, wg_ref, wu_ref, wd_ref):
    xb = xn.astype(jnp.bfloat16)
    g = jnp.dot(xb, wg_ref[...], preferred_element_type=jnp.float32)
    u = jnp.dot(xb, wu_ref[...], preferred_element_type=jnp.float32)
    act = (g * (1.0 / (1.0 + jnp.exp(-g))) * u).astype(jnp.bfloat16)
    return jnp.dot(act, wd_ref[...], preferred_element_type=jnp.float32)


def _ffn_proj_kernel(x_ref, pos_ref, invf_ref, g1_ref, wg_ref, wu_ref, wd_ref, gm_ref,
                     wt_ref, wkb_ref,
                     h_ref, qa_ref, ka_ref, kmean_ref, va_ref, qb_ref, kb_ref, vb_ref):
    x = x_ref[...]
    h = x + FFN_RES * _swiglu(_rms(x, g1_ref[...]), wg_ref, wu_ref, wd_ref)
    h_ref[...] = h
    u = _rms(h, gm_ref[...]).astype(jnp.bfloat16)
    tm = u.shape[0]
    pt = lax.dot_general(wt_ref[...], u, _NT, preferred_element_type=jnp.float32)
    kb = jnp.dot(u, wkb_ref[...], preferred_element_type=jnp.float32)

    ang = invf_ref[...] * pos_ref[...].astype(jnp.float32)
    cos = jnp.cos(ang)[None]
    sin = jnp.sin(ang)[None]

    def rotary(t):
        x1 = t[:, :ROPE_HALF]
        x2 = t[:, ROPE_HALF:ROPE_DIM]
        return jnp.concatenate([x1 * cos - x2 * sin, x2 * cos + x1 * sin, t[:, ROPE_DIM:]], axis=1)

    def heads(lo):
        return pt[lo * D_ATT:(lo + 1) * D_ATT].reshape(N_HEADS, HEAD_DIM, tm)

    qa_ref[:, 0] = rotary(heads(0))
    ka = rotary(heads(1)).reshape(D_ATT, tm).T
    kmean_ref[0] = jnp.mean(ka, axis=0, keepdims=True)
    ka_ref[...] = ka.astype(jnp.bfloat16)
    kb_ref[...] = kb.astype(jnp.bfloat16)
    va_ref[:, 0] = heads(2).astype(jnp.bfloat16)
    qb_ref[:, 0] = (heads(3) * (SCALE * LOG2E)).astype(jnp.bfloat16)
    vb_ref[:, 0] = heads(4).astype(jnp.bfloat16)


def _pad_pair(q, hh):
    z = jnp.zeros_like(q)
    return jnp.concatenate([q, z] if hh % 2 == 0 else [z, q], axis=0)


def _k_rows(k_ref, j, nblk, hh):
    p = hh // 2
    return k_ref[pl.ds(pl.multiple_of(j * BLK, BLK), nblk * BLK), p * LANES:(p + 1) * LANES]


def _moba_kernel(q_ref, k_ref, v_ref, kmean_ref, o_ref, qz_ref, bias_ref, m_ref, acc_ref,
                 sa_ref, sb_ref):
    i = pl.program_id(0)
    nb, tq = bias_ref.shape[1:]
    blk = lax.broadcasted_iota(jnp.int32, (nb, tq), 0)
    heads = range(N_HEADS)

    def v_aug(hh, j, nblk):
        ones = jnp.ones((BF16_ROWS, nblk * BLK), jnp.bfloat16)
        return jnp.concatenate(
            [jnp.concatenate([v_ref[hh, j + u] for u in range(nblk)], axis=1), ones], axis=0)

    q32 = [q_ref[hh, 0] for hh in heads]
    def split(x):
        hi = x.astype(jnp.bfloat16)
        return hi, (x - hi.astype(jnp.float32)).astype(jnp.bfloat16)

    def gate_scores(hh):
        km_hi, km_lo = split(kmean_ref[hh])
        q_hi, q_lo = split(q32[hh])
        return jnp.dot(jnp.concatenate([km_hi, km_hi, km_lo], axis=1),
                       jnp.concatenate([q_hi, q_lo, q_hi], axis=0),
                       preferred_element_type=jnp.float32)

    gates = [gate_scores(hh) for hh in heads]
    for hh in heads:
        qz_ref[hh] = _pad_pair((q32[hh] * (SCALE * LOG2E)).astype(jnp.bfloat16), hh)
    for hh in heads:
        g = jnp.where(blk < i, gates[hh], NEG)
        sel = jnp.zeros((nb, tq), jnp.bool_)
        for _ in range(TOP_K):
            mx = jnp.max(g, axis=0, keepdims=True)
            first = jnp.min(jnp.where(g == mx, blk, nb), axis=0, keepdims=True)
            pick = blk == first
            sel = sel | pick
            g = jnp.where(pick, -jnp.inf, g)
        bias_ref[hh] = jnp.where(sel & (blk < i), 0.0, NEG)

    row = lax.broadcasted_iota(jnp.int32, (BLK, tq), 0)
    col = lax.broadcasted_iota(jnp.int32, (BLK, tq), 1)
    s_own = [jnp.dot(_k_rows(k_ref, i, 1, hh), qz_ref[hh], preferred_element_type=jnp.float32)
             for hh in heads]
    for hh in heads:
        s = jnp.where(row <= col, s_own[hh], NEG)
        m = jnp.max(s, axis=0, keepdims=True)
        p = jnp.exp2(s - m)
        m_ref[hh] = m
        acc_ref[hh] = jnp.dot(v_aug(hh, i, 1), p.astype(jnp.bfloat16),
                              preferred_element_type=jnp.float32)

    n_groups_max = nb // GROUP

    def scores_into(dst_ref, grp, hh):
        dst_ref[hh] = jnp.dot(_k_rows(k_ref, grp * GROUP, GROUP, hh), qz_ref[hh],
                              preferred_element_type=jnp.float32)

    def probs(src_ref, grp, hh):
        j = grp * GROUP
        parts = [src_ref[hh, u * BLK:(u + 1) * BLK] for u in range(GROUP)]
        biases = [bias_ref[hh, pl.ds(j + u, 1), :] for u in range(GROUP)]
        m_old = m_ref[hh]
        m_new = m_old
        for s_u, b_u in zip(parts, biases):
            m_new = jnp.maximum(m_new, jnp.max(s_u, axis=0, keepdims=True) + b_u)
        p = jnp.concatenate([jnp.exp2((s_u - (m_new - b_u)).astype(jnp.bfloat16))
                             for s_u, b_u in zip(parts, biases)], axis=0)
        m_ref[hh] = m_new
        return hh, j, jnp.exp2(m_old - m_new), p

    def accumulate(hh, j, alpha, p):
        acc_ref[hh] = alpha * acc_ref[hh] + jnp.dot(v_aug(hh, j, GROUP), p,
                                                   preferred_element_type=jnp.float32)

    for hh in heads:
        scores_into(sa_ref, 0, hh)

    def walk(g0, phases, prefetch_last):
        pending = None
        for ph in range(phases):
            src_ref, dst_ref = (sa_ref, sb_ref) if ph % 2 == 0 else (sb_ref, sa_ref)
            grp_next = jnp.minimum(g0 + ph + 1, n_groups_max - 1)
            for hh in heads:
                if prefetch_last or ph + 1 < phases:
                    scores_into(dst_ref, grp_next, hh)
                ready = probs(src_ref, g0 + ph, hh)
                if pending is not None:
                    accumulate(*pending)
                pending = ready
        accumulate(*pending)

    def several_groups(t, carry):
        walk(PHASES * t, PHASES, True)
        return carry

    n_groups = (i + GROUP - 1) // GROUP
    rest = n_groups % PHASES
    n_full = n_groups // PHASES + jnp.where(rest > PHASES // 2, 1, 0)
    lax.fori_loop(0, n_full, several_groups, 0)

    @pl.when((rest > 0) & (rest <= PHASES // 2))
    def _():
        walk(PHASES * n_full, PHASES // 2, False)

    for hh in heads:
        acc = acc_ref[hh]
        o_ref[hh * HEAD_DIM:(hh + 1) * HEAD_DIM, :] = (
            acc[:HEAD_DIM] * (1.0 / acc[HEAD_DIM:HEAD_DIM + 1])).astype(o_ref.dtype)


def _sb_kernel(q_ref, k_ref, v_ref, o_ref, qz_ref, c_ref, acc_ref):
    i = pl.program_id(0)
    tq = q_ref.shape[-1]
    row = lax.broadcasted_iota(jnp.int32, (BLK, tq), 0)
    col = lax.broadcasted_iota(jnp.int32, (BLK, tq), 1)
    upper = jnp.where(row < col, 1.0, 0.0).astype(jnp.bfloat16)

    heads = range(N_HEADS)

    def block(j, diag):
        zs = [jnp.dot(_k_rows(k_ref, j, 1, hh), qz_ref[hh], preferred_element_type=jnp.float32)
              for hh in heads]
        log_sigs, log_fails, laters = [], [], []
        for hh in heads:
            z = zs[hh]
            log_sig = jnp.minimum(z, 0.0) - jnp.log2(1.0 + jnp.exp2(-jnp.abs(z)))
            log_fail = log_sig - z
            if diag:
                log_fail = jnp.where(row < col, log_fail, 0.0)
            laters.append(jnp.dot(upper, log_fail.astype(jnp.bfloat16),
                                  preferred_element_type=jnp.float32))
            log_sigs.append(log_sig)
            log_fails.append(log_fail)
        cmax = None
        for hh in heads:
            c = c_ref[hh]
            w = jnp.exp2(log_sigs[hh] + laters[hh] + c)
            if diag:
                w = jnp.where(row < col, w, 0.0)
            acc_ref[hh] += jnp.dot(v_ref[hh, j], w.astype(jnp.bfloat16),
                                   preferred_element_type=jnp.float32)
            c_new = c + jnp.sum(log_fails[hh], axis=0, keepdims=True)
            c_ref[hh] = c_new
            cmax = c_new if cmax is None else jnp.maximum(cmax, c_new)
        return jnp.max(cmax)

    for hh in heads:
        qz_ref[hh] = _pad_pair(q_ref[hh, 0], hh)
    c_ref[...] = jnp.zeros_like(c_ref)
    acc_ref[...] = jnp.zeros_like(acc_ref)
    cmax = block(i, True)

    def cond(state):
        j, cmax = state
        return (j >= 0) & (cmax > SB_LOG2_ZERO)

    def body(state):
        j, _ = state
        return j - 1, block(j, False)

    lax.while_loop(cond, body, (i - 1, cmax))
    for hh in range(N_HEADS):
        o_ref[hh * HEAD_DIM:(hh + 1) * HEAD_DIM, :] = acc_ref[hh].astype(o_ref.dtype)


def _merge_ffn_kernel(h_ref, ya_ref, yb_ref, gm_ref, wgate_ref, bg_ref, wa_ref, wb_ref, wo_ref,
                      g2_ref, wg_ref, wu_ref, wd_ref, gf_ref, o_ref):
    h = h_ref[...]
    d = h.shape[-1]
    u = _rms(h, gm_ref[...]).astype(jnp.bfloat16)
    gl = jnp.dot(u, wgate_ref[...], preferred_element_type=jnp.float32) + bg_ref[...]
    gates = 1.0 / (1.0 + jnp.exp(-gl))
    ma = lax.dot_general(ya_ref[...], wa_ref[...], _TN, preferred_element_type=jnp.float32)
    mb = lax.dot_general(yb_ref[...], wb_ref[...], _TN, preferred_element_type=jnp.float32)
    merged = gates[:, :d] * ma + gates[:, d:] * mb
    h = h + jnp.dot(merged.astype(jnp.bfloat16), wo_ref[...], preferred_element_type=jnp.float32)
    h = h + FFN_RES * _swiglu(_rms(h, g2_ref[...]), wg_ref, wu_ref, wd_ref)
    o_ref[...] = _rms(h, gf_ref[...])


def _resident(shape):
    return pl.BlockSpec(shape, lambda *_: (0,) * len(shape), pipeline_mode=pl.Buffered(1))


def kernel(x, positions, ffn1_norm, ffn1_w_gate, ffn1_w_up, ffn1_w_down, mix_norm, w_in, b_gate,
           w_branch_moba, w_branch_sb, w_out, ffn2_norm, ffn2_w_gate, ffn2_w_up, ffn2_w_down,
           final_norm):
    batch, seq, d = x.shape
    assert batch == 1 and ffn1_norm.shape[0] == 1 and seq % BLK == 0 and d == w_out.shape[-1]
    nb = seq // BLK
    d_ff = ffn1_w_gate.shape[-1]
    bf = jnp.bfloat16
    f32 = jnp.float32
    x2 = x[0]
    pos = positions.reshape(1, seq)
    w_in0 = w_in[0]
    a, b = D_ATT, 3 * D_ATT
    w_t = jnp.concatenate([w_in0[:, :b], w_in0[:, b:b + a], w_in0[:, b + 2 * a:b + 3 * a]], axis=1).T.astype(bf)
    w_kb = w_in0[:, b + a:b + 2 * a].astype(bf)
    w_gates = w_in0[:, 2 * b:].astype(bf)
    inv_freq = (ROPE_THETA ** (-jnp.arange(ROPE_HALF, dtype=f32) * 2.0 / ROPE_DIM)).reshape(ROPE_HALF, 1)
    row = lambda v: v.reshape(1, -1)

    cparams = functools.partial(pltpu.CompilerParams, vmem_limit_bytes=VMEM_LIMIT_BYTES)
    tok = lambda i: (i, 0)
    head_t = pl.BlockSpec((N_HEADS, 1, HEAD_DIM, BLK), lambda i: (0, i, 0, 0))
    t_shape = (N_HEADS, nb, HEAD_DIM, BLK)

    h1, qa, ka, kmean, va, qb, kb, vb = pl.pallas_call(
        _ffn_proj_kernel,
        grid=(nb,),
        in_specs=[
            pl.BlockSpec((BLK, d), tok),
            pl.BlockSpec((1, BLK), lambda i: (0, i)),
            _resident((ROPE_HALF, 1)),
            _resident((1, d)),
            _resident((d, d_ff)), _resident((d, d_ff)), _resident((d_ff, d)),
            _resident((1, d)),
            _resident((5 * D_ATT, d)), _resident((d, D_ATT)),
        ],
        out_specs=[
            pl.BlockSpec((BLK, d), tok),
            head_t,
            pl.BlockSpec((BLK, D_ATT), tok),
            pl.BlockSpec((1, 1, D_ATT), lambda i: (i, 0, 0)),
            head_t, head_t,
            pl.BlockSpec((BLK, D_ATT), tok),
            head_t,
        ],
        out_shape=[
            jax.ShapeDtypeStruct((seq, d), f32),
            jax.ShapeDtypeStruct(t_shape, f32),
            jax.ShapeDtypeStruct((seq, D_ATT), bf),
            jax.ShapeDtypeStruct((nb, 1, D_ATT), f32),
            jax.ShapeDtypeStruct(t_shape, bf),
            jax.ShapeDtypeStruct(t_shape, bf),
            jax.ShapeDtypeStruct((seq, D_ATT), bf),
            jax.ShapeDtypeStruct(t_shape, bf),
        ],
        compiler_params=cparams(dimension_semantics=("arbitrary",)),
        name="ffn1_proj",
    )(x2, pos, inv_freq, row(ffn1_norm[0]), ffn1_w_gate[0].astype(bf), ffn1_w_up[0].astype(bf),
      ffn1_w_down[0].astype(bf), row(mix_norm[0]), w_t, w_kb)

    kmean_h = kmean.reshape(nb, N_HEADS, HEAD_DIM).transpose(1, 0, 2)

    k_all = _resident((seq, D_ATT))
    v_all = _resident(t_shape)
    y_spec = pl.BlockSpec((D_ATT, BLK), lambda i: (0, i))
    y_shape = jax.ShapeDtypeStruct((D_ATT, seq), bf)
    qz_scratch = pltpu.VMEM((N_HEADS, 2 * HEAD_DIM, BLK), bf)

    ya = pl.pallas_call(
        _moba_kernel,
        grid=(nb,),
        in_specs=[head_t, k_all, v_all, _resident((N_HEADS, nb, HEAD_DIM))],
        out_specs=y_spec,
        out_shape=y_shape,
        scratch_shapes=[qz_scratch,
                        pltpu.VMEM((N_HEADS, nb, BLK), f32),
                        pltpu.VMEM((N_HEADS, 1, BLK), f32),
                        pltpu.VMEM((N_HEADS, HEAD_DIM + BF16_ROWS, BLK), f32),
                        pltpu.VMEM((N_HEADS, GROUP * BLK, BLK), f32),
                        pltpu.VMEM((N_HEADS, GROUP * BLK, BLK), f32)],
        compiler_params=cparams(dimension_semantics=("arbitrary",)),
        name="moba_attention",
    )(qa, ka, va, kmean_h)

    yb = pl.pallas_call(
        _sb_kernel,
        grid=(nb,),
        in_specs=[head_t, k_all, v_all],
        out_specs=y_spec,
        out_shape=y_shape,
        scratch_shapes=[qz_scratch,
                        pltpu.VMEM((N_HEADS, 1, BLK), f32),
                        pltpu.VMEM((N_HEADS, HEAD_DIM, BLK), f32)],
        compiler_params=cparams(dimension_semantics=("arbitrary",)),
        name="stick_breaking_attention",
    )(qb, kb, vb)

    out = pl.pallas_call(
        _merge_ffn_kernel,
        grid=(nb,),
        in_specs=[
            pl.BlockSpec((BLK, d), tok),
            pl.BlockSpec((D_ATT, BLK), lambda i: (0, i)),
            pl.BlockSpec((D_ATT, BLK), lambda i: (0, i)),
            _resident((1, d)),
            _resident((d, 2 * d)), _resident((1, 2 * d)),
            _resident((D_ATT, d)), _resident((D_ATT, d)), _resident((d, d)),
            _resident((1, d)),
            _resident((d, d_ff)), _resident((d, d_ff)), _resident((d_ff, d)),
            _resident((1, d)),
        ],
        out_specs=pl.BlockSpec((BLK, d), tok),
        out_shape=jax.ShapeDtypeStruct((seq, d), f32),
        compiler_params=cparams(dimension_semantics=("arbitrary",)),
        name="merge_ffn2",
    )(h1, ya, yb, row(mix_norm[0]), w_gates, row(b_gate[0]), w_branch_moba[0].astype(bf),
      w_branch_sb[0].astype(bf), w_out[0].astype(bf), row(ffn2_norm[0]), ffn2_w_gate[0].astype(bf),
      ffn2_w_up[0].astype(bf), ffn2_w_down[0].astype(bf), row(final_norm))
    return out[None]
```

```python
import functools

import jax
import jax.numpy as jnp
from jax import lax
from jax.experimental import pallas as pl
from jax.experimental.pallas import tpu as pltpu

HEAD_DIM = 64
N_HEADS = 8
D_ATT = N_HEADS * HEAD_DIM
LANES = 128
BF16_ROWS = 16
BLK = 256
TOP_K = 3
GROUP = 1
TOK_SUBTILES = 2
PHASES = 4
ROPE_THETA = 500000.0
ROPE_DIM = HEAD_DIM // 4
ROPE_HALF = ROPE_DIM // 2
FFN_RES = 0.5
EPS = 1e-6
NEG = -1e30
SCALE = HEAD_DIM ** -0.5
LOG2E = 1.4426950408889634
SB_LOG2_ZERO = -151.0
VMEM_LIMIT_BYTES = 60 * 1024 * 1024

_NT = (((1,), (1,)), ((), ()))
_TN = (((0,), (0,)), ((), ()))


def _rms(x, g):
    return x * lax.rsqrt(jnp.mean(x * x, axis=-1, keepdims=True) + EPS) * g


def _swiglu(xn, wg_ref, wu_ref, wd_ref):
    xb = xn.astype(jnp.bfloat16)
    g = jnp.dot(xb, wg_ref[...], preferred_element_type=jnp.float32)
    u = jnp.dot(xb, wu_ref[...], preferred_element_type=jnp.float32)
    act = (g * (1.0 / (1.0 + jnp.exp(-g))) * u).astype(jnp.bfloat16)
    return jnp.dot(act, wd_ref[...], preferred_element_type=jnp.float32)


def _ffn_proj_kernel(x_ref, pos_ref, invf_ref, g1_ref, wg_ref, wu_ref, wd_ref, gm_ref,
                     wt_ref, wkb_ref,
                     h_ref, qa_ref, ka_ref, kmean_ref, va_ref, qb_ref, kb_ref, vb_ref):
    subs = range(x_ref.shape[0] // BLK)
    rows = [pl.ds(c * BLK, BLK) for c in subs]
    xs = [x_ref[r, :] for r in rows]
    xn = [_rms(x, g1_ref[...]) for x in xs]
    hs = [x + FFN_RES * _swiglu(n, wg_ref, wu_ref, wd_ref) for x, n in zip(xs, xn)]
    us = []
    for r, h in zip(rows, hs):
        h_ref[r, :] = h
        us.append(_rms(h, gm_ref[...]).astype(jnp.bfloat16))
    pts = [lax.dot_general(wt_ref[...], u, _NT, preferred_element_type=jnp.float32) for u in us]
    kbs = [jnp.dot(u, wkb_ref[...], preferred_element_type=jnp.float32) for u in us]

    for c, r, pt, kb in zip(subs, rows, pts, kbs):
        ang = invf_ref[...] * pos_ref[:, r].astype(jnp.float32)
        cos = jnp.cos(ang)[None]
        sin = jnp.sin(ang)[None]

        def rotary(t):
            x1 = t[:, :ROPE_HALF]
            x2 = t[:, ROPE_HALF:ROPE_DIM]
            return jnp.concatenate([x1 * cos - x2 * sin, x2 * cos + x1 * sin, t[:, ROPE_DIM:]],
                                   axis=1)

        def heads(lo):
            return pt[lo * D_ATT:(lo + 1) * D_ATT].reshape(N_HEADS, HEAD_DIM, BLK)

        qa_ref[:, c] = rotary(heads(0))
        ka = rotary(heads(1)).reshape(D_ATT, BLK).T
        kmean_ref[c] = jnp.mean(ka, axis=0, keepdims=True)
        ka_ref[r, :] = ka.astype(jnp.bfloat16)
        kb_ref[r, :] = kb.astype(jnp.bfloat16)
        va_ref[:, c] = heads(2).astype(jnp.bfloat16)
        qb_ref[:, c] = (heads(3) * (SCALE * LOG2E)).astype(jnp.bfloat16)
        vb_ref[:, c] = heads(4).astype(jnp.bfloat16)


def _pad_pair(q, hh):
    z = jnp.zeros_like(q)
    return jnp.concatenate([q, z] if hh % 2 == 0 else [z, q], axis=0)


def _k_rows(k_ref, j, nblk, hh):
    p = hh // 2
    return k_ref[pl.ds(pl.multiple_of(j * BLK, BLK), nblk * BLK), p * LANES:(p + 1) * LANES]


def _moba_kernel(q_ref, k_ref, v_ref, kmean_ref, o_ref, qz_ref, bias_ref, m_ref, acc_ref,
                 sa_ref, sb_ref):
    i = pl.program_id(0)
    nb, tq = bias_ref.shape[1:]
    blk = lax.broadcasted_iota(jnp.int32, (nb, tq), 0)
    heads = range(N_HEADS)

    def v_aug(hh, j, nblk):
        ones = jnp.ones((BF16_ROWS, nblk * BLK), jnp.bfloat16)
        return jnp.concatenate(
            [jnp.concatenate([v_ref[hh, j + u] for u in range(nblk)], axis=1), ones], axis=0)

    q32 = [q_ref[hh, 0] for hh in heads]
    def split(x):
        hi = x.astype(jnp.bfloat16)
        return hi, (x - hi.astype(jnp.float32)).astype(jnp.bfloat16)

    def gate_scores(hh):
        km_hi, km_lo = split(kmean_ref[hh])
        q_hi, q_lo = split(q32[hh])
        return jnp.dot(jnp.concatenate([km_hi, km_hi, km_lo], axis=1),
                       jnp.concatenate([q_hi, q_lo, q_hi], axis=0),
                       preferred_element_type=jnp.float32)

    for hh in heads:
        qz_ref[hh] = _pad_pair((q32[hh] * (SCALE * LOG2E)).astype(jnp.bfloat16), hh)
    for hh in heads:
        sa_ref[hh] = jnp.dot(_k_rows(k_ref, 0, GROUP, hh), qz_ref[hh],
                             preferred_element_type=jnp.float32)
    s_own = [jnp.dot(_k_rows(k_ref, i, 1, hh), qz_ref[hh], preferred_element_type=jnp.float32)
             for hh in heads]
    gates = [gate_scores(hh) for hh in heads]
    for hh in heads:
        g = jnp.where(blk < i, gates[hh], NEG)
        sel = jnp.zeros((nb, tq), jnp.bool_)
        for _ in range(TOP_K):
            mx = jnp.max(g, axis=0, keepdims=True)
            first = jnp.min(jnp.where(g == mx, blk, nb), axis=0, keepdims=True)
            pick = blk == first
            sel = sel | pick
            g = jnp.where(pick, -jnp.inf, g)
        bias_ref[hh] = jnp.where(sel & (blk < i), 0.0, NEG)

    row = lax.broadcasted_iota(jnp.int32, (BLK, tq), 0)
    col = lax.broadcasted_iota(jnp.int32, (BLK, tq), 1)
    for hh in heads:
        s = jnp.where(row <= col, s_own[hh], NEG)
        m = jnp.max(s, axis=0, keepdims=True)
        p = jnp.exp2(s - m)
        m_ref[hh] = m
        acc_ref[hh] = jnp.dot(v_aug(hh, i, 1), p.astype(jnp.bfloat16),
                              preferred_element_type=jnp.float32)

    n_groups_max = nb // GROUP

    def scores_into(dst_ref, grp, hh):
        dst_ref[hh] = jnp.dot(_k_rows(k_ref, grp * GROUP, GROUP, hh), qz_ref[hh],
                              preferred_element_type=jnp.float32)

    def probs(src_ref, grp, hh):
        j = grp * GROUP
        parts = [src_ref[hh, u * BLK:(u + 1) * BLK] for u in range(GROUP)]
        biases = [bias_ref[hh, pl.ds(j + u, 1), :] for u in range(GROUP)]
        m_old = m_ref[hh]
        m_new = m_old
        for s_u, b_u in zip(parts, biases):
            m_new = jnp.maximum(m_new, jnp.max(s_u, axis=0, keepdims=True) + b_u)
        p = jnp.concatenate([jnp.exp2((s_u - (m_new - b_u)).astype(jnp.bfloat16))
                             for s_u, b_u in zip(parts, biases)], axis=0)
        m_ref[hh] = m_new
        return hh, j, jnp.exp2(m_old - m_new), p

    def accumulate(hh, j, alpha, p):
        acc_ref[hh] = alpha * acc_ref[hh] + jnp.dot(v_aug(hh, j, GROUP), p,
                                                   preferred_element_type=jnp.float32)

    def walk(g0, phases, prefetch_last):
        pending = None
        for ph in range(phases):
            src_ref, dst_ref = (sa_ref, sb_ref) if ph % 2 == 0 else (sb_ref, sa_ref)
            grp_next = jnp.minimum(g0 + ph + 1, n_groups_max - 1)
            for hh in heads:
                if prefetch_last or ph + 1 < phases:
                    scores_into(dst_ref, grp_next, hh)
                ready = probs(src_ref, g0 + ph, hh)
                if pending is not None:
                    accumulate(*pending)
                pending = ready
        accumulate(*pending)

    def several_groups(t, carry):
        walk(PHASES * t, PHASES, True)
        return carry

    n_groups = (i + GROUP - 1) // GROUP
    rest = n_groups % PHASES
    n_full = n_groups // PHASES + jnp.where(rest > PHASES // 2, 1, 0)
    lax.fori_loop(0, n_full, several_groups, 0)

    @pl.when((rest > 0) & (rest <= PHASES // 2))
    def _():
        walk(PHASES * n_full, PHASES // 2, False)

    for hh in heads:
        acc = acc_ref[hh]
        o_ref[hh * HEAD_DIM:(hh + 1) * HEAD_DIM, :] = (
            acc[:HEAD_DIM] * (1.0 / acc[HEAD_DIM:HEAD_DIM + 1])).astype(o_ref.dtype)


def _sb_kernel(q_ref, k_ref, v_ref, o_ref, qz_ref, c_ref, acc_ref):
    i = pl.program_id(0)
    tq = q_ref.shape[-1]
    row = lax.broadcasted_iota(jnp.int32, (BLK, tq), 0)
    col = lax.broadcasted_iota(jnp.int32, (BLK, tq), 1)
    upper = jnp.where(row < col, 1.0, 0.0).astype(jnp.bfloat16)

    heads = range(N_HEADS)

    def block(j, diag):
        zs = [jnp.dot(_k_rows(k_ref, j, 1, hh), qz_ref[hh], preferred_element_type=jnp.float32)
              for hh in heads]
        log_sigs, log_fails, laters = [], [], []
        for hh in heads:
            z = zs[hh]
            log_sig = jnp.minimum(z, 0.0) - jnp.log2(1.0 + jnp.exp2(-jnp.abs(z)))
            log_fail = log_sig - z
            if diag:
                log_fail = jnp.where(row < col, log_fail, 0.0)
            laters.append(jnp.dot(upper, log_fail.astype(jnp.bfloat16),
                                  preferred_element_type=jnp.float32))
            log_sigs.append(log_sig)
            log_fails.append(log_fail)
        cmax = None
        for hh in heads:
            c = c_ref[hh]
            w = jnp.exp2(log_sigs[hh] + laters[hh] + c)
            if diag:
                w = jnp.where(row < col, w, 0.0)
            acc_ref[hh] += jnp.dot(v_ref[hh, j], w.astype(jnp.bfloat16),
                                   preferred_element_type=jnp.float32)
            c_new = c + jnp.sum(log_fails[hh], axis=0, keepdims=True)
            c_ref[hh] = c_new
            cmax = c_new if cmax is None else jnp.maximum(cmax, c_new)
        return jnp.max(cmax)

    for hh in heads:
        qz_ref[hh] = _pad_pair(q_ref[hh, 0], hh)
    c_ref[...] = jnp.zeros_like(c_ref)
    acc_ref[...] = jnp.zeros_like(acc_ref)
    cmax = block(i, True)

    def cond(state):
        j, cmax = state
        return (j >= 0) & (cmax > SB_LOG2_ZERO)

    def body(state):
        j, _ = state
        return j - 1, block(j, False)

    lax.while_loop(cond, body, (i - 1, cmax))
    for hh in range(N_HEADS):
        o_ref[hh * HEAD_DIM:(hh + 1) * HEAD_DIM, :] = acc_ref[hh].astype(o_ref.dtype)


def _merge_ffn_kernel(h_ref, ya_ref, yb_ref, gm_ref, wgate_ref, bg_ref, wa_ref, wb_ref, wo_ref,
                      g2_ref, wg_ref, wu_ref, wd_ref, gf_ref, o_ref):
    d = h_ref.shape[-1]
    subs = range(h_ref.shape[0] // BLK)
    rows = [pl.ds(c * BLK, BLK) for c in subs]
    hs = [h_ref[r, :] for r in rows]
    us = [_rms(h, gm_ref[...]).astype(jnp.bfloat16) for h in hs]
    gls = [jnp.dot(u, wgate_ref[...], preferred_element_type=jnp.float32) + bg_ref[...] for u in us]
    mas = [lax.dot_general(ya_ref[:, r], wa_ref[...], _TN, preferred_element_type=jnp.float32)
           for r in rows]
    mbs = [lax.dot_general(yb_ref[:, r], wb_ref[...], _TN, preferred_element_type=jnp.float32)
           for r in rows]
    merged = []
    for gl, ma, mb in zip(gls, mas, mbs):
        gates = 1.0 / (1.0 + jnp.exp(-gl))
        merged.append((gates[:, :d] * ma + gates[:, d:] * mb).astype(jnp.bfloat16))
    hs = [h + jnp.dot(m, wo_ref[...], preferred_element_type=jnp.float32)
          for h, m in zip(hs, merged)]
    xn = [_rms(h, g2_ref[...]) for h in hs]
    hs = [h + FFN_RES * _swiglu(n, wg_ref, wu_ref, wd_ref) for h, n in zip(hs, xn)]
    for r, h in zip(rows, hs):
        o_ref[r, :] = _rms(h, gf_ref[...])


def _resident(shape):
    return pl.BlockSpec(shape, lambda *_: (0,) * len(shape), pipeline_mode=pl.Buffered(1))


def kernel(x, positions, ffn1_norm, ffn1_w_gate, ffn1_w_up, ffn1_w_down, mix_norm, w_in, b_gate,
           w_branch_moba, w_branch_sb, w_out, ffn2_norm, ffn2_w_gate, ffn2_w_up, ffn2_w_down,
           final_norm):
    batch, seq, d = x.shape
    assert batch == 1 and ffn1_norm.shape[0] == 1 and seq % BLK == 0 and d == w_out.shape[-1]
    nb = seq // BLK
    d_ff = ffn1_w_gate.shape[-1]
    bf = jnp.bfloat16
    f32 = jnp.float32
    x2 = x[0]
    pos = positions.reshape(1, seq)
    w_in0 = w_in[0]
    a, b = D_ATT, 3 * D_ATT
    w_t = jnp.concatenate([w_in0[:, :b], w_in0[:, b:b + a], w_in0[:, b + 2 * a:b + 3 * a]], axis=1).T.astype(bf)
    w_kb = w_in0[:, b + a:b + 2 * a].astype(bf)
    w_gates = w_in0[:, 2 * b:].astype(bf)
    inv_freq = (ROPE_THETA ** (-jnp.arange(ROPE_HALF, dtype=f32) * 2.0 / ROPE_DIM)).reshape(ROPE_HALF, 1)
    row = lambda v: v.reshape(1, -1)

    cparams = functools.partial(pltpu.CompilerParams, vmem_limit_bytes=VMEM_LIMIT_BYTES)
    tok = lambda i: (i, 0)
    head_t = pl.BlockSpec((N_HEADS, 1, HEAD_DIM, BLK), lambda i: (0, i, 0, 0))
    t_shape = (N_HEADS, nb, HEAD_DIM, BLK)

    tile = TOK_SUBTILES * BLK
    assert seq % tile == 0
    head_tile = pl.BlockSpec((N_HEADS, TOK_SUBTILES, HEAD_DIM, BLK), lambda i: (0, i, 0, 0))

    h1, qa, ka, kmean, va, qb, kb, vb = pl.pallas_call(
        _ffn_proj_kernel,
        grid=(seq // tile,),
        in_specs=[
            pl.BlockSpec((tile, d), tok),
            pl.BlockSpec((1, tile), lambda i: (0, i)),
            _resident((ROPE_HALF, 1)),
            _resident((1, d)),
            _resident((d, d_ff)), _resident((d, d_ff)), _resident((d_ff, d)),
            _resident((1, d)),
            _resident((5 * D_ATT, d)), _resident((d, D_ATT)),
        ],
        out_specs=[
            pl.BlockSpec((tile, d), tok),
            head_tile,
            pl.BlockSpec((tile, D_ATT), tok),
            pl.BlockSpec((TOK_SUBTILES, 1, D_ATT), lambda i: (i, 0, 0)),
            head_tile, head_tile,
            pl.BlockSpec((tile, D_ATT), tok),
            head_tile,
        ],
        out_shape=[
            jax.ShapeDtypeStruct((seq, d), f32),
            jax.ShapeDtypeStruct(t_shape, f32),
            jax.ShapeDtypeStruct((seq, D_ATT), bf),
            jax.ShapeDtypeStruct((nb, 1, D_ATT), f32),
            jax.ShapeDtypeStruct(t_shape, bf),
            jax.ShapeDtypeStruct(t_shape, bf),
            jax.ShapeDtypeStruct((seq, D_ATT), bf),
            jax.ShapeDtypeStruct(t_shape, bf),
        ],
        compiler_params=cparams(dimension_semantics=("arbitrary",)),
        name="ffn1_proj",
    )(x2, pos, inv_freq, row(ffn1_norm[0]), ffn1_w_gate[0].astype(bf), ffn1_w_up[0].astype(bf),
      ffn1_w_down[0].astype(bf), row(mix_norm[0]), w_t, w_kb)

    kmean_h = kmean.reshape(nb, N_HEADS, HEAD_DIM).transpose(1, 0, 2)

    k_all = _resident((seq, D_ATT))
    v_all = _resident(t_shape)
    y_spec = pl.BlockSpec((D_ATT, BLK), lambda i: (0, i))
    y_shape = jax.ShapeDtypeStruct((D_ATT, seq), bf)
    qz_scratch = pltpu.VMEM((N_HEADS, 2 * HEAD_DIM, BLK), bf)

    ya = pl.pallas_call(
        _moba_kernel,
        grid=(nb,),
        in_specs=[head_t, k_all, v_all, _resident((N_HEADS, nb, HEAD_DIM))],
        out_specs=y_spec,
        out_shape=y_shape,
        scratch_shapes=[qz_scratch,
                        pltpu.VMEM((N_HEADS, nb, BLK), f32),
                        pltpu.VMEM((N_HEADS, 1, BLK), f32),
                        pltpu.VMEM((N_HEADS, HEAD_DIM + BF16_ROWS, BLK), f32),
                        pltpu.VMEM((N_HEADS, GROUP * BLK, BLK), f32),
                        pltpu.VMEM((N_HEADS, GROUP * BLK, BLK), f32)],
        compiler_params=cparams(dimension_semantics=("arbitrary",)),
        name="moba_attention",
    )(qa, ka, va, kmean_h)

    yb = pl.pallas_call(
        _sb_kernel,
        grid=(nb,),
        in_specs=[head_t, k_all, v_all],
        out_specs=y_spec,
        out_shape=y_shape,
        scratch_shapes=[qz_scratch,
                        pltpu.VMEM((N_HEADS, 1, BLK), f32),
                        pltpu.VMEM((N_HEADS, HEAD_DIM, BLK), f32)],
        compiler_params=cparams(dimension_semantics=("arbitrary",)),
        name="stick_breaking_attention",
    )(qb, kb, vb)

    out = pl.pallas_call(
        _merge_ffn_kernel,
        grid=(seq // tile,),
        in_specs=[
            pl.BlockSpec((tile, d), tok),
            pl.BlockSpec((D_ATT, tile), lambda i: (0, i)),
            pl.BlockSpec((D_ATT, tile), lambda i: (0, i)),
            _resident((1, d)),
            _resident((d, 2 * d)), _resident((1, 2 * d)),
            _resident((D_ATT, d)), _resident((D_ATT, d)), _resident((d, d)),
            _resident((1, d)),
            _resident((d, d_ff)), _resident((d, d_ff)), _resident((d_ff, d)),
            _resident((1, d)),
        ],
        out_specs=pl.BlockSpec((tile, d), tok),
        out_shape=jax.ShapeDtypeStruct((seq, d), f32),
        compiler_params=cparams(dimension_semantics=("arbitrary",)),
        name="merge_ffn2",
    )(h1, ya, yb, row(mix_norm[0]), w_gates, row(b_gate[0]), w_branch_moba[0].astype(bf),
      w_branch_sb[0].astype(bf), w_out[0].astype(bf), row(ffn2_norm[0]), ffn2_w_gate[0].astype(bf),
      ffn2_w_up[0].astype(bf), ffn2_w_down[0].astype(bf), row(final_norm))
    return out[None]
```

```python
import functools

import jax
import jax.numpy as jnp
from jax import lax
from jax.experimental import pallas as pl
from jax.experimental.pallas import tpu as pltpu

HEAD_DIM = 64
N_HEADS = 8
D_ATT = N_HEADS * HEAD_DIM
LANES = 128
BF16_ROWS = 16
BLK = 256
TOP_K = 3
GROUP = 1
TOK_SUBTILES = 2
PHASES = 4
ROPE_THETA = 500000.0
ROPE_DIM = HEAD_DIM // 4
ROPE_HALF = ROPE_DIM // 2
FFN_RES = 0.5
EPS = 1e-6
NEG = -1e30
SCALE = HEAD_DIM ** -0.5
LOG2E = 1.4426950408889634
SB_LOG2_ZERO = -151.0
V7X_VMEM_BYTES = 64 * 1024 * 1024
VMEM_LIMIT_BYTES = V7X_VMEM_BYTES - 4 * 1024 * 1024

_NT = (((1,), (1,)), ((), ()))
_TN = (((0,), (0,)), ((), ()))


def _rms(x, g):
    return x * lax.rsqrt(jnp.mean(x * x, axis=-1, keepdims=True) + EPS) * g


def _swiglu(xn, wg_ref, wu_ref, wd_ref):
    xb = xn.astype(jnp.bfloat16)
    g = jnp.dot(xb, wg_ref[...], preferred_element_type=jnp.float32)
    u = jnp.dot(xb, wu_ref[...], preferred_element_type=jnp.float32)
    act = (g * (1.0 / (1.0 + jnp.exp(-g))) * u).astype(jnp.bfloat16)
    return jnp.dot(act, wd_ref[...], preferred_element_type=jnp.float32)


def _ffn_proj_kernel(x_ref, pos_ref, invf_ref, g1_ref, wg_ref, wu_ref, wd_ref, gm_ref,
                     wt_ref, wkb_ref,
                     h_ref, qa_ref, ka_ref, kmean_ref, va_ref, qb_ref, kb_ref, vb_ref):
    subs = range(x_ref.shape[0] // BLK)
    rows = [pl.ds(c * BLK, BLK) for c in subs]
    xs = [x_ref[r, :] for r in rows]
    xn = [_rms(x, g1_ref[...]) for x in xs]
    hs = [x + FFN_RES * _swiglu(n, wg_ref, wu_ref, wd_ref) for x, n in zip(xs, xn)]
    us = []
    for r, h in zip(rows, hs):
        h_ref[r, :] = h
        us.append(_rms(h, gm_ref[...]).astype(jnp.bfloat16))
    pts = [lax.dot_general(wt_ref[...], u, _NT, preferred_element_type=jnp.float32) for u in us]
    kbs = [jnp.dot(u, wkb_ref[...], preferred_element_type=jnp.float32) for u in us]

    for c, r, pt, kb in zip(subs, rows, pts, kbs):
        ang = invf_ref[...] * pos_ref[:, r].astype(jnp.float32)
        cos = jnp.cos(ang)[None]
        sin = jnp.sin(ang)[None]

        def rotary(t):
            x1 = t[:, :ROPE_HALF]
            x2 = t[:, ROPE_HALF:ROPE_DIM]
            return jnp.concatenate([x1 * cos - x2 * sin, x2 * cos + x1 * sin, t[:, ROPE_DIM:]],
                                   axis=1)

        def heads(lo):
            return pt[lo * D_ATT:(lo + 1) * D_ATT].reshape(N_HEADS, HEAD_DIM, BLK)

        qa_ref[:, c] = rotary(heads(0))
        ka = rotary(heads(1)).reshape(D_ATT, BLK).T
        kmean_ref[c] = jnp.mean(ka, axis=0, keepdims=True)
        ka_ref[r, :] = ka.astype(jnp.bfloat16)
        kb_ref[r, :] = kb.astype(jnp.bfloat16)
        va_ref[:, c] = heads(2).astype(jnp.bfloat16)
        qb_ref[:, c] = (heads(3) * (SCALE * LOG2E)).astype(jnp.bfloat16)
        vb_ref[:, c] = heads(4).astype(jnp.bfloat16)


def _pad_pair(q, hh):
    z = jnp.zeros_like(q)
    return jnp.concatenate([q, z] if hh % 2 == 0 else [z, q], axis=0)


def _k_rows(k_ref, j, nblk, hh):
    p = hh // 2
    return k_ref[pl.ds(pl.multiple_of(j * BLK, BLK), nblk * BLK), p * LANES:(p + 1) * LANES]


def _moba_kernel(q_ref, k_ref, v_ref, kmean_ref, o_ref, qz_ref, bias_ref, m_ref, acc_ref,
                 sa_ref, sb_ref):
    i = pl.program_id(0)
    nb, tq = bias_ref.shape[1:]
    blk = lax.broadcasted_iota(jnp.int32, (nb, tq), 0)
    blk_f = blk.astype(jnp.float32)
    heads = range(N_HEADS)

    def v_aug(hh, j, nblk):
        ones = jnp.ones((BF16_ROWS, nblk * BLK), jnp.bfloat16)
        return jnp.concatenate(
            [jnp.concatenate([v_ref[hh, j + u] for u in range(nblk)], axis=1), ones], axis=0)

    q32 = [q_ref[hh, 0] for hh in heads]
    def split(x):
        hi = x.astype(jnp.bfloat16)
        return hi, (x - hi.astype(jnp.float32)).astype(jnp.bfloat16)

    def gate_scores(hh):
        km_hi, km_lo = split(kmean_ref[hh])
        q_hi, q_lo = split(q32[hh])
        return jnp.dot(jnp.concatenate([km_hi, km_hi, km_lo], axis=1),
                       jnp.concatenate([q_hi, q_lo, q_hi], axis=0),
                       preferred_element_type=jnp.float32)

    for hh in heads:
        qz_ref[hh] = _pad_pair((q32[hh] * (SCALE * LOG2E)).astype(jnp.bfloat16), hh)
    gates = [gate_scores(hh) for hh in heads]
    for hh in heads:
        sa_ref[hh] = jnp.dot(_k_rows(k_ref, 0, GROUP, hh), qz_ref[hh],
                             preferred_element_type=jnp.float32)
    s_own = [jnp.dot(_k_rows(k_ref, i, 1, hh), qz_ref[hh], preferred_element_type=jnp.float32)
             for hh in heads]
    for hh in heads:
        g = jnp.where(blk < i, gates[hh], NEG)
        bias = jnp.full((nb, tq), NEG, jnp.float32)
        for _ in range(TOP_K):
            mx = jnp.max(g, axis=0, keepdims=True)
            first = jnp.min(jnp.where(g == mx, blk_f, float(nb)), axis=0, keepdims=True)
            pick = blk_f == first
            bias = jnp.where(pick & (blk < i), 0.0, bias)
            g = jnp.where(pick, -jnp.inf, g)
        bias_ref[hh] = bias

    row = lax.broadcasted_iota(jnp.int32, (BLK, tq), 0)
    col = lax.broadcasted_iota(jnp.int32, (BLK, tq), 1)
    for hh in heads:
        s = jnp.where(row <= col, s_own[hh], NEG)
        m = jnp.max(s, axis=0, keepdims=True)
        p = jnp.exp2(s - m)
        m_ref[hh] = m
        acc_ref[hh] = jnp.dot(v_aug(hh, i, 1), p.astype(jnp.bfloat16),
                              preferred_element_type=jnp.float32)

    n_groups_max = nb // GROUP

    def scores_into(dst_ref, grp, hh):
        dst_ref[hh] = jnp.dot(_k_rows(k_ref, grp * GROUP, GROUP, hh), qz_ref[hh],
                              preferred_element_type=jnp.float32)

    def probs(src_ref, grp, hh):
        j = grp * GROUP
        parts = [src_ref[hh, u * BLK:(u + 1) * BLK] for u in range(GROUP)]
        biases = [bias_ref[hh, pl.ds(j + u, 1), :] for u in range(GROUP)]
        m_old = m_ref[hh]
        m_new = m_old
        for s_u, b_u in zip(parts, biases):
            m_new = jnp.maximum(m_new, jnp.max(s_u, axis=0, keepdims=True) + b_u)
        p = jnp.concatenate([jnp.exp2((s_u - (m_new - b_u)).astype(jnp.bfloat16))
                             for s_u, b_u in zip(parts, biases)], axis=0)
        m_ref[hh] = m_new
        return hh, j, jnp.exp2(m_old - m_new), p

    def accumulate(hh, j, alpha, p):
        acc_ref[hh] = alpha * acc_ref[hh] + jnp.dot(v_aug(hh, j, GROUP), p,
                                                   preferred_element_type=jnp.float32)

    def walk(g0, phases, prefetch_last):
        pending = None
        for ph in range(phases):
            src_ref, dst_ref = (sa_ref, sb_ref) if ph % 2 == 0 else (sb_ref, sa_ref)
            grp_next = jnp.minimum(g0 + ph + 1, n_groups_max - 1)
            for hh in heads:
                if prefetch_last or ph + 1 < phases:
                    scores_into(dst_ref, grp_next, hh)
                ready = probs(src_ref, g0 + ph, hh)
                if pending is not None:
                    accumulate(*pending)
                pending = ready
        accumulate(*pending)

    def several_groups(t, carry):
        walk(PHASES * t, PHASES, True)
        return carry

    n_groups = (i + GROUP - 1) // GROUP
    rest = n_groups % PHASES
    n_full = n_groups // PHASES + jnp.where(rest > PHASES // 2, 1, 0)
    lax.fori_loop(0, n_full, several_groups, 0)

    @pl.when((rest > 0) & (rest <= PHASES // 2))
    def _():
        walk(PHASES * n_full, PHASES // 2, False)

    for hh in heads:
        acc = acc_ref[hh]
        o_ref[hh * HEAD_DIM:(hh + 1) * HEAD_DIM, :] = (
            acc[:HEAD_DIM] * (1.0 / acc[HEAD_DIM:HEAD_DIM + 1])).astype(o_ref.dtype)


def _sb_kernel(q_ref, k_ref, v_ref, o_ref, qz_ref, c_ref, acc_ref):
    i = pl.program_id(0)
    tq = q_ref.shape[-1]
    row = lax.broadcasted_iota(jnp.int32, (BLK, tq), 0)
    col = lax.broadcasted_iota(jnp.int32, (BLK, tq), 1)
    upper = jnp.where(row < col, 1.0, 0.0).astype(jnp.bfloat16)

    heads = range(N_HEADS)

    def block(j, diag):
        zs = [jnp.dot(_k_rows(k_ref, j, 1, hh), qz_ref[hh], preferred_element_type=jnp.float32)
              for hh in heads]
        log_sigs, log_fails, laters = [], [], []
        for hh in heads:
            z = zs[hh]
            log_sig = jnp.minimum(z, 0.0) - jnp.log2(1.0 + jnp.exp2(-jnp.abs(z)))
            log_fail = log_sig - z
            if diag:
                log_fail = jnp.where(row < col, log_fail, 0.0)
            laters.append(jnp.dot(upper, log_fail.astype(jnp.bfloat16),
                                  preferred_element_type=jnp.float32))
            log_sigs.append(log_sig)
            log_fails.append(log_fail)
        cmax = None
        for hh in heads:
            c = c_ref[hh]
            w = jnp.exp2(log_sigs[hh] + laters[hh] + c)
            if diag:
                w = jnp.where(row < col, w, 0.0)
            acc_ref[hh] += jnp.dot(v_ref[hh, j], w.astype(jnp.bfloat16),
                                   preferred_element_type=jnp.float32)
            c_new = c + jnp.sum(log_fails[hh], axis=0, keepdims=True)
            c_ref[hh] = c_new
            cmax = c_new if cmax is None else jnp.maximum(cmax, c_new)
        return jnp.max(cmax)

    for hh in heads:
        qz_ref[hh] = _pad_pair(q_ref[hh, 0], hh)
    c_ref[...] = jnp.zeros_like(c_ref)
    acc_ref[...] = jnp.zeros_like(acc_ref)
    cmax = block(i, True)

    def cond(state):
        j, cmax = state
        return (j >= 0) & (cmax > SB_LOG2_ZERO)

    def body(state):
        j, _ = state
        return j - 1, block(j, False)

    lax.while_loop(cond, body, (i - 1, cmax))
    for hh in range(N_HEADS):
        o_ref[hh * HEAD_DIM:(hh + 1) * HEAD_DIM, :] = acc_ref[hh].astype(o_ref.dtype)


def _merge_ffn_kernel(h_ref, ya_ref, yb_ref, gm_ref, wgate_ref, bg_ref, wa_ref, wb_ref, wo_ref,
                      g2_ref, wg_ref, wu_ref, wd_ref, gf_ref, o_ref):
    d = h_ref.shape[-1]
    subs = range(h_ref.shape[0] // BLK)
    rows = [pl.ds(c * BLK, BLK) for c in subs]
    hs = [h_ref[r, :] for r in rows]
    us = [_rms(h, gm_ref[...]).astype(jnp.bfloat16) for h in hs]
    gls = [jnp.dot(u, wgate_ref[...], preferred_element_type=jnp.float32) + bg_ref[...] for u in us]
    mas = [lax.dot_general(ya_ref[:, r], wa_ref[...], _TN, preferred_element_type=jnp.float32)
           for r in rows]
    mbs = [lax.dot_general(yb_ref[:, r], wb_ref[...], _TN, preferred_element_type=jnp.float32)
           for r in rows]
    merged = []
    for gl, ma, mb in zip(gls, mas, mbs):
        gates = 1.0 / (1.0 + jnp.exp(-gl))
        merged.append((gates[:, :d] * ma + gates[:, d:] * mb).astype(jnp.bfloat16))
    hs = [h + jnp.dot(m, wo_ref[...], preferred_element_type=jnp.float32)
          for h, m in zip(hs, merged)]
    xn = [_rms(h, g2_ref[...]) for h in hs]
    hs = [h + FFN_RES * _swiglu(n, wg_ref, wu_ref, wd_ref) for h, n in zip(hs, xn)]
    for r, h in zip(rows, hs):
        o_ref[r, :] = _rms(h, gf_ref[...])


def _resident(shape):
    return pl.BlockSpec(shape, lambda *_: (0,) * len(shape), pipeline_mode=pl.Buffered(1))


def kernel(x, positions, ffn1_norm, ffn1_w_gate, ffn1_w_up, ffn1_w_down, mix_norm, w_in, b_gate,
           w_branch_moba, w_branch_sb, w_out, ffn2_norm, ffn2_w_gate, ffn2_w_up, ffn2_w_down,
           final_norm):
    batch, seq, d = x.shape
    assert batch == 1 and ffn1_norm.shape[0] == 1 and seq % BLK == 0 and d == w_out.shape[-1]
    nb = seq // BLK
    d_ff = ffn1_w_gate.shape[-1]
    bf = jnp.bfloat16
    f32 = jnp.float32
    x2 = x[0]
    pos = positions.reshape(1, seq)
    w_in0 = w_in[0]
    a, b = D_ATT, 3 * D_ATT
    w_t = jnp.concatenate([w_in0[:, :b], w_in0[:, b:b + a], w_in0[:, b + 2 * a:b + 3 * a]], axis=1).T.astype(bf)
    w_kb = w_in0[:, b + a:b + 2 * a].astype(bf)
    w_gates = w_in0[:, 2 * b:].astype(bf)
    inv_freq = (ROPE_THETA ** (-jnp.arange(ROPE_HALF, dtype=f32) * 2.0 / ROPE_DIM)).reshape(ROPE_HALF, 1)
    row = lambda v: v.reshape(1, -1)

    cparams = functools.partial(pltpu.CompilerParams, vmem_limit_bytes=VMEM_LIMIT_BYTES)
    tok = lambda i: (i, 0)
    head_t = pl.BlockSpec((N_HEADS, 1, HEAD_DIM, BLK), lambda i: (0, i, 0, 0))
    t_shape = (N_HEADS, nb, HEAD_DIM, BLK)

    tile = TOK_SUBTILES * BLK
    assert seq % tile == 0
    head_tile = pl.BlockSpec((N_HEADS, TOK_SUBTILES, HEAD_DIM, BLK), lambda i: (0, i, 0, 0))

    h1, qa, ka, kmean, va, qb, kb, vb = pl.pallas_call(
        _ffn_proj_kernel,
        grid=(seq // tile,),
        in_specs=[
            pl.BlockSpec((tile, d), tok),
            pl.BlockSpec((1, tile), lambda i: (0, i)),
            _resident((ROPE_HALF, 1)),
            _resident((1, d)),
            _resident((d, d_ff)), _resident((d, d_ff)), _resident((d_ff, d)),
            _resident((1, d)),
            _resident((5 * D_ATT, d)), _resident((d, D_ATT)),
        ],
        out_specs=[
            pl.BlockSpec((tile, d), tok),
            head_tile,
            pl.BlockSpec((tile, D_ATT), tok),
            pl.BlockSpec((TOK_SUBTILES, 1, D_ATT), lambda i: (i, 0, 0)),
            head_tile, head_tile,
            pl.BlockSpec((tile, D_ATT), tok),
            head_tile,
        ],
        out_shape=[
            jax.ShapeDtypeStruct((seq, d), f32),
            jax.ShapeDtypeStruct(t_shape, f32),
            jax.ShapeDtypeStruct((seq, D_ATT), bf),
            jax.ShapeDtypeStruct((nb, 1, D_ATT), f32),
            jax.ShapeDtypeStruct(t_shape, bf),
            jax.ShapeDtypeStruct(t_shape, bf),
            jax.ShapeDtypeStruct((seq, D_ATT), bf),
            jax.ShapeDtypeStruct(t_shape, bf),
        ],
        compiler_params=cparams(dimension_semantics=("arbitrary",)),
        name="ffn1_proj",
    )(x2, pos, inv_freq, row(ffn1_norm[0]), ffn1_w_gate[0].astype(bf), ffn1_w_up[0].astype(bf),
      ffn1_w_down[0].astype(bf), row(mix_norm[0]), w_t, w_kb)

    kmean_h = kmean.reshape(nb, N_HEADS, HEAD_DIM).transpose(1, 0, 2)

    k_all = _resident((seq, D_ATT))
    v_all = _resident(t_shape)
    y_spec = pl.BlockSpec((D_ATT, BLK), lambda i: (0, i))
    y_shape = jax.ShapeDtypeStruct((D_ATT, seq), bf)
    qz_scratch = pltpu.VMEM((N_HEADS, 2 * HEAD_DIM, BLK), bf)

    ya = pl.pallas_call(
        _moba_kernel,
        grid=(nb,),
        in_specs=[head_t, k_all, v_all, _resident((N_HEADS, nb, HEAD_DIM))],
        out_specs=y_spec,
        out_shape=y_shape,
        scratch_shapes=[qz_scratch,
                        pltpu.VMEM((N_HEADS, nb, BLK), f32),
                        pltpu.VMEM((N_HEADS, 1, BLK), f32),
                        pltpu.VMEM((N_HEADS, HEAD_DIM + BF16_ROWS, BLK), f32),
                        pltpu.VMEM((N_HEADS, GROUP * BLK, BLK), f32),
                        pltpu.VMEM((N_HEADS, GROUP * BLK, BLK), f32)],
        compiler_params=cparams(dimension_semantics=("arbitrary",)),
        name="moba_attention",
    )(qa, ka, va, kmean_h)

    yb = pl.pallas_call(
        _sb_kernel,
        grid=(nb,),
        in_specs=[head_t, k_all, v_all],
        out_specs=y_spec,
        out_shape=y_shape,
        scratch_shapes=[qz_scratch,
                        pltpu.VMEM((N_HEADS, 1, BLK), f32),
                        pltpu.VMEM((N_HEADS, HEAD_DIM, BLK), f32)],
        compiler_params=cparams(dimension_semantics=("arbitrary",)),
        name="stick_breaking_attention",
    )(qb, kb, vb)

    out = pl.pallas_call(
        _merge_ffn_kernel,
        grid=(seq // tile,),
        in_specs=[
            pl.BlockSpec((tile, d), tok),
            pl.BlockSpec((D_ATT, tile), lambda i: (0, i)),
            pl.BlockSpec((D_ATT, tile), lambda i: (0, i)),
            _resident((1, d)),
            _resident((d, 2 * d)), _resident((1, 2 * d)),
            _resident((D_ATT, d)), _resident((D_ATT, d)), _resident((d, d)),
            _resident((1, d)),
            _resident((d, d_ff)), _resident((d, d_ff)), _resident((d_ff, d)),
            _resident((1, d)),
        ],
        out_specs=pl.BlockSpec((tile, d), tok),
        out_shape=jax.ShapeDtypeStruct((seq, d), f32),
        compiler_params=cparams(dimension_semantics=("arbitrary",)),
        name="merge_ffn2",
    )(h1, ya, yb, row(mix_norm[0]), w_gates, row(b_gate[0]), w_branch_moba[0].astype(bf),
      w_branch_sb[0].astype(bf), w_out[0].astype(bf), row(ffn2_norm[0]), ffn2_w_gate[0].astype(bf),
      ffn2_w_up[0].astype(bf), ffn2_w_down[0].astype(bf), row(final_norm))
    return out[None]
```

```python
import functools

import jax
import jax.numpy as jnp
from jax import lax
from jax.experimental import pallas as pl
from jax.experimental.pallas import tpu as pltpu

HEAD_DIM = 64
N_HEADS = 8
D_ATT = N_HEADS * HEAD_DIM
LANES = 128
BF16_ROWS = 16
BLK = 256
TOP_K = 3
GROUP = 1
TOK_SUBTILES = 2
PHASES = 6
ROPE_THETA = 500000.0
ROPE_DIM = HEAD_DIM // 4
ROPE_HALF = ROPE_DIM // 2
FFN_RES = 0.5
EPS = 1e-6
NEG = -1e30
SCALE = HEAD_DIM ** -0.5
LOG2E = 1.4426950408889634
SB_LOG2_ZERO = -151.0
V7X_VMEM_BYTES = 64 * 1024 * 1024
VMEM_LIMIT_BYTES = V7X_VMEM_BYTES - 4 * 1024 * 1024

_NT = (((1,), (1,)), ((), ()))
_TN = (((0,), (0,)), ((), ()))


def _rms(x, g):
    return x * lax.rsqrt(jnp.mean(x * x, axis=-1, keepdims=True) + EPS) * g


def _swiglu(xn, wg_ref, wu_ref, wd_ref):
    xb = xn.astype(jnp.bfloat16)
    g = jnp.dot(xb, wg_ref[...], preferred_element_type=jnp.float32)
    u = jnp.dot(xb, wu_ref[...], preferred_element_type=jnp.float32)
    act = (g * (1.0 / (1.0 + jnp.exp(-g))) * u).astype(jnp.bfloat16)
    return jnp.dot(act, wd_ref[...], preferred_element_type=jnp.float32)


def _ffn_proj_kernel(x_ref, pos_ref, invf_ref, g1_ref, wg_ref, wu_ref, wd_ref, gm_ref,
                     wt_ref, wkb_ref,
                     h_ref, qa_ref, ka_ref, kmean_ref, va_ref, qb_ref, kb_ref, vb_ref):
    subs = range(x_ref.shape[0] // BLK)
    rows = [pl.ds(c * BLK, BLK) for c in subs]
    xs = [x_ref[r, :] for r in rows]
    xn = [_rms(x, g1_ref[...]) for x in xs]
    hs = [x + FFN_RES * _swiglu(n, wg_ref, wu_ref, wd_ref) for x, n in zip(xs, xn)]
    us = []
    for r, h in zip(rows, hs):
        h_ref[r, :] = h
        us.append(_rms(h, gm_ref[...]).astype(jnp.bfloat16))
    pts = [lax.dot_general(wt_ref[...], u, _NT, preferred_element_type=jnp.float32) for u in us]
    kbs = [jnp.dot(u, wkb_ref[...], preferred_element_type=jnp.float32) for u in us]

    for c, r, pt, kb in zip(subs, rows, pts, kbs):
        ang = invf_ref[...] * pos_ref[:, r].astype(jnp.float32)
        cos = jnp.cos(ang)[None]
        sin = jnp.sin(ang)[None]

        def rotary(t):
            x1 = t[:, :ROPE_HALF]
            x2 = t[:, ROPE_HALF:ROPE_DIM]
            return jnp.concatenate([x1 * cos - x2 * sin, x2 * cos + x1 * sin, t[:, ROPE_DIM:]],
                                   axis=1)

        def heads(lo):
            return pt[lo * D_ATT:(lo + 1) * D_ATT].reshape(N_HEADS, HEAD_DIM, BLK)

        qa_ref[:, c] = rotary(heads(0))
        ka = rotary(heads(1)).reshape(D_ATT, BLK).T
        kmean_ref[c] = jnp.mean(ka, axis=0, keepdims=True)
        ka_ref[r, :] = ka.astype(jnp.bfloat16)
        kb_ref[r, :] = kb.astype(jnp.bfloat16)
        va_ref[:, c] = heads(2).astype(jnp.bfloat16)
        qb_ref[:, c] = (heads(3) * (SCALE * LOG2E)).astype(jnp.bfloat16)
        vb_ref[:, c] = heads(4).astype(jnp.bfloat16)


def _pad_pair(q, hh):
    z = jnp.zeros_like(q)
    return jnp.concatenate([q, z] if hh % 2 == 0 else [z, q], axis=0)


def _k_rows(k_ref, j, nblk, hh):
    p = hh // 2
    return k_ref[pl.ds(pl.multiple_of(j * BLK, BLK), nblk * BLK), p * LANES:(p + 1) * LANES]


def _moba_kernel(q_ref, k_ref, v_ref, kmean_ref, o_ref, qz_ref, bias_ref, m_ref, acc_ref,
                 sa_ref, sb_ref):
    i = pl.program_id(0)
    nb, tq = bias_ref.shape[1:]
    blk = lax.broadcasted_iota(jnp.int32, (nb, tq), 0)
    blk_f = blk.astype(jnp.float32)
    heads = range(N_HEADS)

    def v_aug(hh, j, nblk):
        ones = jnp.ones((BF16_ROWS, nblk * BLK), jnp.bfloat16)
        return jnp.concatenate(
            [jnp.concatenate([v_ref[hh, j + u] for u in range(nblk)], axis=1), ones], axis=0)

    q32 = [q_ref[hh, 0] for hh in heads]
    def split(x):
        hi = x.astype(jnp.bfloat16)
        return hi, (x - hi.astype(jnp.float32)).astype(jnp.bfloat16)

    def gate_scores(hh):
        km_hi, km_lo = split(kmean_ref[hh])
        q_hi, q_lo = split(q32[hh])
        return jnp.dot(jnp.concatenate([km_hi, km_hi, km_lo], axis=1),
                       jnp.concatenate([q_hi, q_lo, q_hi], axis=0),
                       preferred_element_type=jnp.float32)

    for hh in heads:
        qz_ref[hh] = _pad_pair((q32[hh] * (SCALE * LOG2E)).astype(jnp.bfloat16), hh)
    gates = [gate_scores(hh) for hh in heads]
    for hh in heads:
        sa_ref[hh] = jnp.dot(_k_rows(k_ref, 0, GROUP, hh), qz_ref[hh],
                             preferred_element_type=jnp.float32)
    s_own = [jnp.dot(_k_rows(k_ref, i, 1, hh), qz_ref[hh], preferred_element_type=jnp.float32)
             for hh in heads]
    for hh in heads:
        g = jnp.where(blk < i, gates[hh], NEG)
        bias = jnp.full((nb, tq), NEG, jnp.float32)
        for _ in range(TOP_K):
            mx = jnp.max(g, axis=0, keepdims=True)
            first = jnp.min(jnp.where(g == mx, blk_f, float(nb)), axis=0, keepdims=True)
            pick = blk_f == first
            bias = jnp.where(pick & (blk < i), 0.0, bias)
            g = jnp.where(pick, -jnp.inf, g)
        bias_ref[hh] = bias

    row = lax.broadcasted_iota(jnp.int32, (BLK, tq), 0)
    col = lax.broadcasted_iota(jnp.int32, (BLK, tq), 1)
    for hh in heads:
        s = jnp.where(row <= col, s_own[hh], NEG)
        m = jnp.max(s, axis=0, keepdims=True)
        p = jnp.exp2(s - m)
        m_ref[hh] = m
        acc_ref[hh] = jnp.dot(v_aug(hh, i, 1), p.astype(jnp.bfloat16),
                              preferred_element_type=jnp.float32)

    n_groups_max = nb // GROUP

    def scores_into(dst_ref, grp, hh):
        dst_ref[hh] = jnp.dot(_k_rows(k_ref, grp * GROUP, GROUP, hh), qz_ref[hh],
                              preferred_element_type=jnp.float32)

    def probs(src_ref, grp, hh):
        j = grp * GROUP
        parts = [src_ref[hh, u * BLK:(u + 1) * BLK] for u in range(GROUP)]
        biases = [bias_ref[hh, pl.ds(j + u, 1), :] for u in range(GROUP)]
        m_old = m_ref[hh]
        m_new = m_old
        for s_u, b_u in zip(parts, biases):
            m_new = jnp.maximum(m_new, jnp.max(s_u, axis=0, keepdims=True) + b_u)
        p = jnp.concatenate([jnp.exp2((s_u - (m_new - b_u)).astype(jnp.bfloat16))
                             for s_u, b_u in zip(parts, biases)], axis=0)
        m_ref[hh] = m_new
        return hh, j, jnp.exp2(m_old - m_new), p

    def accumulate(hh, j, alpha, p):
        acc_ref[hh] = alpha * acc_ref[hh] + jnp.dot(v_aug(hh, j, GROUP), p,
                                                   preferred_element_type=jnp.float32)

    def walk(g0, phases, prefetch_last):
        pending = None
        for ph in range(phases):
            src_ref, dst_ref = (sa_ref, sb_ref) if ph % 2 == 0 else (sb_ref, sa_ref)
            grp_next = jnp.minimum(g0 + ph + 1, n_groups_max - 1)
            for hh in heads:
                if prefetch_last or ph + 1 < phases:
                    scores_into(dst_ref, grp_next, hh)
                ready = probs(src_ref, g0 + ph, hh)
                if pending is not None:
                    accumulate(*pending)
                pending = ready
        accumulate(*pending)

    def several_groups(t, carry):
        walk(PHASES * t, PHASES, True)
        return carry

    n_groups = (i + GROUP - 1) // GROUP
    rest = n_groups % PHASES
    n_full = n_groups // PHASES + jnp.where(rest > PHASES // 2, 1, 0)
    lax.fori_loop(0, n_full, several_groups, 0)

    @pl.when((rest > 0) & (rest <= PHASES // 2))
    def _():
        walk(PHASES * n_full, PHASES // 2, False)

    for hh in heads:
        acc = acc_ref[hh]
        o_ref[hh * HEAD_DIM:(hh + 1) * HEAD_DIM, :] = (
            acc[:HEAD_DIM] * (1.0 / acc[HEAD_DIM:HEAD_DIM + 1])).astype(o_ref.dtype)


def _sb_kernel(q_ref, k_ref, v_ref, o_ref, qz_ref, c_ref, acc_ref):
    i = pl.program_id(0)
    tq = q_ref.shape[-1]
    row = lax.broadcasted_iota(jnp.int32, (BLK, tq), 0)
    col = lax.broadcasted_iota(jnp.int32, (BLK, tq), 1)
    upper = jnp.where(row < col, 1.0, 0.0).astype(jnp.bfloat16)

    heads = range(N_HEADS)

    def block(j, diag):
        zs = [jnp.dot(_k_rows(k_ref, j, 1, hh), qz_ref[hh], preferred_element_type=jnp.float32)
              for hh in heads]
        log_sigs, log_fails, laters = [], [], []
        for hh in heads:
            z = zs[hh]
            log_sig = jnp.minimum(z, 0.0) - jnp.log2(1.0 + jnp.exp2(-jnp.abs(z)))
            log_fail = log_sig - z
            if diag:
                log_fail = jnp.where(row < col, log_fail, 0.0)
            laters.append(jnp.dot(upper, log_fail.astype(jnp.bfloat16),
                                  preferred_element_type=jnp.float32))
            log_sigs.append(log_sig)
            log_fails.append(log_fail)
        cmax = None
        for hh in heads:
            c = c_ref[hh]
            w = jnp.exp2(log_sigs[hh] + laters[hh] + c)
            if diag:
                w = jnp.where(row < col, w, 0.0)
            acc_ref[hh] += jnp.dot(v_ref[hh, j], w.astype(jnp.bfloat16),
                                   preferred_element_type=jnp.float32)
            c_new = c + jnp.sum(log_fails[hh], axis=0, keepdims=True)
            c_ref[hh] = c_new
            cmax = c_new if cmax is None else jnp.maximum(cmax, c_new)
        return jnp.max(cmax)

    for hh in heads:
        qz_ref[hh] = _pad_pair(q_ref[hh, 0], hh)
    c_ref[...] = jnp.zeros_like(c_ref)
    acc_ref[...] = jnp.zeros_like(acc_ref)
    cmax = block(i, True)

    def cond(state):
        j, cmax = state
        return (j >= 0) & (cmax > SB_LOG2_ZERO)

    def body(state):
        j, _ = state
        return j - 1, block(j, False)

    lax.while_loop(cond, body, (i - 1, cmax))
    for hh in range(N_HEADS):
        o_ref[hh * HEAD_DIM:(hh + 1) * HEAD_DIM, :] = acc_ref[hh].astype(o_ref.dtype)


def _merge_ffn_kernel(h_ref, ya_ref, yb_ref, gm_ref, wgate_ref, bg_ref, wa_ref, wb_ref, wo_ref,
                      g2_ref, wg_ref, wu_ref, wd_ref, gf_ref, o_ref):
    d = h_ref.shape[-1]
    subs = range(h_ref.shape[0] // BLK)
    rows = [pl.ds(c * BLK, BLK) for c in subs]
    hs = [h_ref[r, :] for r in rows]
    us = [_rms(h, gm_ref[...]).astype(jnp.bfloat16) for h in hs]
    gls = [jnp.dot(u, wgate_ref[...], preferred_element_type=jnp.float32) + bg_ref[...] for u in us]
    mas = [lax.dot_general(ya_ref[:, r], wa_ref[...], _TN, preferred_element_type=jnp.float32)
           for r in rows]
    mbs = [lax.dot_general(yb_ref[:, r], wb_ref[...], _TN, preferred_element_type=jnp.float32)
           for r in rows]
    merged = []
    for gl, ma, mb in zip(gls, mas, mbs):
        gates = 1.0 / (1.0 + jnp.exp(-gl))
        merged.append((gates[:, :d] * ma + gates[:, d:] * mb).astype(jnp.bfloat16))
    hs = [h + jnp.dot(m, wo_ref[...], preferred_element_type=jnp.float32)
          for h, m in zip(hs, merged)]
    xn = [_rms(h, g2_ref[...]) for h in hs]
    hs = [h + FFN_RES * _swiglu(n, wg_ref, wu_ref, wd_ref) for h, n in zip(hs, xn)]
    for r, h in zip(rows, hs):
        o_ref[r, :] = _rms(h, gf_ref[...])


def _resident(shape):
    return pl.BlockSpec(shape, lambda *_: (0,) * len(shape), pipeline_mode=pl.Buffered(1))


def kernel(x, positions, ffn1_norm, ffn1_w_gate, ffn1_w_up, ffn1_w_down, mix_norm, w_in, b_gate,
           w_branch_moba, w_branch_sb, w_out, ffn2_norm, ffn2_w_gate, ffn2_w_up, ffn2_w_down,
           final_norm):
    batch, seq, d = x.shape
    assert batch == 1 and ffn1_norm.shape[0] == 1 and seq % BLK == 0 and d == w_out.shape[-1]
    nb = seq // BLK
    d_ff = ffn1_w_gate.shape[-1]
    bf = jnp.bfloat16
    f32 = jnp.float32
    x2 = x[0]
    pos = positions.reshape(1, seq)
    w_in0 = w_in[0]
    a, b = D_ATT, 3 * D_ATT
    w_t = jnp.concatenate([w_in0[:, :b], w_in0[:, b:b + a], w_in0[:, b + 2 * a:b + 3 * a]], axis=1).T.astype(bf)
    w_kb = w_in0[:, b + a:b + 2 * a].astype(bf)
    w_gates = w_in0[:, 2 * b:].astype(bf)
    inv_freq = (ROPE_THETA ** (-jnp.arange(ROPE_HALF, dtype=f32) * 2.0 / ROPE_DIM)).reshape(ROPE_HALF, 1)
    row = lambda v: v.reshape(1, -1)

    cparams = functools.partial(pltpu.CompilerParams, vmem_limit_bytes=VMEM_LIMIT_BYTES)
    tok = lambda i: (i, 0)
    head_t = pl.BlockSpec((N_HEADS, 1, HEAD_DIM, BLK), lambda i: (0, i, 0, 0))
    t_shape = (N_HEADS, nb, HEAD_DIM, BLK)

    tile = TOK_SUBTILES * BLK
    assert seq % tile == 0
    head_tile = pl.BlockSpec((N_HEADS, TOK_SUBTILES, HEAD_DIM, BLK), lambda i: (0, i, 0, 0))

    h1, qa, ka, kmean, va, qb, kb, vb = pl.pallas_call(
        _ffn_proj_kernel,
        grid=(seq // tile,),
        in_specs=[
            pl.BlockSpec((tile, d), tok),
            pl.BlockSpec((1, tile), lambda i: (0, i)),
            _resident((ROPE_HALF, 1)),
            _resident((1, d)),
            _resident((d, d_ff)), _resident((d, d_ff)), _resident((d_ff, d)),
            _resident((1, d)),
            _resident((5 * D_ATT, d)), _resident((d, D_ATT)),
        ],
        out_specs=[
            pl.BlockSpec((tile, d), tok),
            head_tile,
            pl.BlockSpec((tile, D_ATT), tok),
            pl.BlockSpec((TOK_SUBTILES, 1, D_ATT), lambda i: (i, 0, 0)),
            head_tile, head_tile,
            pl.BlockSpec((tile, D_ATT), tok),
            head_tile,
        ],
        out_shape=[
            jax.ShapeDtypeStruct((seq, d), f32),
            jax.ShapeDtypeStruct(t_shape, f32),
            jax.ShapeDtypeStruct((seq, D_ATT), bf),
            jax.ShapeDtypeStruct((nb, 1, D_ATT), f32),
            jax.ShapeDtypeStruct(t_shape, bf),
            jax.ShapeDtypeStruct(t_shape, bf),
            jax.ShapeDtypeStruct((seq, D_ATT), bf),
            jax.ShapeDtypeStruct(t_shape, bf),
        ],
        compiler_params=cparams(dimension_semantics=("arbitrary",)),
        name="ffn1_proj",
    )(x2, pos, inv_freq, row(ffn1_norm[0]), ffn1_w_gate[0].astype(bf), ffn1_w_up[0].astype(bf),
      ffn1_w_down[0].astype(bf), row(mix_norm[0]), w_t, w_kb)

    kmean_h = kmean.reshape(nb, N_HEADS, HEAD_DIM).transpose(1, 0, 2)

    k_all = _resident((seq, D_ATT))
    v_all = _resident(t_shape)
    y_spec = pl.BlockSpec((D_ATT, BLK), lambda i: (0, i))
    y_shape = jax.ShapeDtypeStruct((D_ATT, seq), bf)
    qz_scratch = pltpu.VMEM((N_HEADS, 2 * HEAD_DIM, BLK), bf)

    ya = pl.pallas_call(
        _moba_kernel,
        grid=(nb,),
        in_specs=[head_t, k_all, v_all, _resident((N_HEADS, nb, HEAD_DIM))],
        out_specs=y_spec,
        out_shape=y_shape,
        scratch_shapes=[qz_scratch,
                        pltpu.VMEM((N_HEADS, nb, BLK), f32),
                        pltpu.VMEM((N_HEADS, 1, BLK), f32),
                        pltpu.VMEM((N_HEADS, HEAD_DIM + BF16_ROWS, BLK), f32),
                        pltpu.VMEM((N_HEADS, GROUP * BLK, BLK), f32),
                        pltpu.VMEM((N_HEADS, GROUP * BLK, BLK), f32)],
        compiler_params=cparams(dimension_semantics=("arbitrary",)),
        name="moba_attention",
    )(qa, ka, va, kmean_h)

    yb = pl.pallas_call(
        _sb_kernel,
        grid=(nb,),
        in_specs=[head_t, k_all, v_all],
        out_specs=y_spec,
        out_shape=y_shape,
        scratch_shapes=[qz_scratch,
                        pltpu.VMEM((N_HEADS, 1, BLK), f32),
                        pltpu.VMEM((N_HEADS, HEAD_DIM, BLK), f32)],
        compiler_params=cparams(dimension_semantics=("arbitrary",)),
        name="stick_breaking_attention",
    )(qb, kb, vb)

    out = pl.pallas_call(
        _merge_ffn_kernel,
        grid=(seq // tile,),
        in_specs=[
            pl.BlockSpec((tile, d), tok),
            pl.BlockSpec((D_ATT, tile), lambda i: (0, i)),
            pl.BlockSpec((D_ATT, tile), lambda i: (0, i)),
            _resident((1, d)),
            _resident((d, 2 * d)), _resident((1, 2 * d)),
            _resident((D_ATT, d)), _resident((D_ATT, d)), _resident((d, d)),
            _resident((1, d)),
            _resident((d, d_ff)), _resident((d, d_ff)), _resident((d_ff, d)),
            _resident((1, d)),
        ],
        out_specs=pl.BlockSpec((tile, d), tok),
        out_shape=jax.ShapeDtypeStruct((seq, d), f32),
        compiler_params=cparams(dimension_semantics=("arbitrary",)),
        name="merge_ffn2",
    )(h1, ya, yb, row(mix_norm[0]), w_gates, row(b_gate[0]), w_branch_moba[0].astype(bf),
      w_branch_sb[0].astype(bf), w_out[0].astype(bf), row(ffn2_norm[0]), ffn2_w_gate[0].astype(bf),
      ffn2_w_up[0].astype(bf), ffn2_w_down[0].astype(bf), row(final_norm))
    return out[None]
```
